```python
import math
import jax, jax.numpy as jnp
from jax import lax
import numpy as np

D_MODEL = 2048
BATCH = 2
SEQ = 4096
DEPTH = 2
DEC_BATCH = 32
DEC_SEQ = 4
PAST_LEN = 8192
PAGE_SIZE = 128

SSM_HEAD_DIM = 64
SSM_HEADS = D_MODEL // SSM_HEAD_DIM
SSM_D_INNER = SSM_HEADS * SSM_HEAD_DIM
SSM_GROUPS = 4
SSM_STATE = 128
SSM_CONV = 4
SSM_CHUNK = 128
SSM_CONV_DIM = SSM_D_INNER + 2 * SSM_GROUPS * SSM_STATE
CCV_DIM = D_MODEL // 2
CCV_WIDTH = 31
AB_IN = SSM_D_INNER + SSM_CONV_DIM + SSM_HEADS + 2 * CCV_DIM
AB_MIX = SSM_D_INNER + CCV_DIM
FOX_HEADS = 16
FOX_HEAD_DIM = D_MODEL // FOX_HEADS
FOX_WIDTH = FOX_HEADS * FOX_HEAD_DIM
FOX_IN = 3 * FOX_WIDTH + FOX_HEADS
Q_BLOCK = 128
MOE_GROUPS = 4
MOE_EXPERTS_PER_GROUP = 4
MOE_EXPERTS = MOE_GROUPS * MOE_EXPERTS_PER_GROUP
MOE_TOP_K = 2
MOE_FF = 512
N_AB_LAYERS = (DEPTH + 1) // 2
N_ATT_LAYERS = DEPTH // 2
EPS = 1e-6

kernel_name = "ssd_conformer_fox_hmoe_adaln_step"


def rmsnorm(x, g):
    xf = x.astype(jnp.float32)
    y = xf * lax.rsqrt(jnp.mean(xf * xf, axis=-1, keepdims=True) + EPS)
    return y.astype(x.dtype) * g


def layernorm(x, g, b):
    xf = x.astype(jnp.float32)
    mu = jnp.mean(xf, axis=-1, keepdims=True)
    var = jnp.mean(jnp.square(xf - mu), axis=-1, keepdims=True)
    return ((xf - mu) * lax.rsqrt(var + EPS)).astype(x.dtype) * g + b


def ada_params(c, w, b):
    mod = jax.nn.silu(c) @ w + b
    shift, scale, gate = jnp.split(mod, 3, axis=-1)
    return shift[:, None, :], scale[:, None, :], gate[:, None, :]


def modulate(x, g, shift, scale):
    return rmsnorm(x, g) * (1 + scale) + shift


def causal_dwconv(x_pad, w, b):
    ch = x_pad.shape[-1]
    y = lax.conv_general_dilated(
        x_pad, w[:, None, :].astype(x_pad.dtype), window_strides=(1,), padding="VALID",
        dimension_numbers=("NWC", "WIO", "NWC"), feature_group_count=ch)
    return y + b


def gated_group_rmsnorm(y, z, g):
    v = (y * jax.nn.silu(z)).astype(jnp.float32)
    sh = v.shape
    v = v.reshape(sh[:-1] + (SSM_GROUPS, sh[-1] // SSM_GROUPS))
    v = v * lax.rsqrt(jnp.mean(v * v, axis=-1, keepdims=True) + EPS)
    return v.reshape(sh).astype(y.dtype) * g


def ssd_scan(x, dt, a_neg, bm, cm, init_state):
    b, l, h, p = x.shape
    g, n = bm.shape[2], bm.shape[3]
    j = h // g
    q = min(SSM_CHUNK, l)
    pad = (-l) % q
    if pad:
        padf = lambda t: jnp.pad(t, [(0, 0), (0, pad)] + [(0, 0)] * (t.ndim - 2))
        x, dt, bm, cm = padf(x), padf(dt), padf(bm), padf(cm)
    c = (l + pad) // q
    xd = (x * dt[..., None]).reshape(b, c, q, g, j, p)
    a = jnp.moveaxis((dt * a_neg).reshape(b, c, q, g, j), 2, -1)
    a_cum = jnp.cumsum(a, axis=-1)
    bc = bm.reshape(b, c, q, g, n)
    cc = cm.reshape(b, c, q, g, n)
    causal = jnp.tril(jnp.ones((q, q), dtype=bool))
    seg = a_cum[..., :, None] - a_cum[..., None, :]
    lmat = jnp.where(causal, jnp.exp(jnp.where(causal, seg, 0.0)), 0.0)
    cb = jnp.einsum('bclgn,bcsgn->bcgls', cc, bc)
    y_diag = jnp.einsum('bcgls,bcgjls,bcsgjp->bclgjp', cb, lmat, xd)
    decay_to_end = jnp.exp(a_cum[..., -1:] - a_cum)
    states = jnp.einsum('bcsgn,bcgjs,bcsgjp->bcgjpn', bc, decay_to_end, xd)
    chunk_decay = jnp.exp(a_cum[..., -1])

    def step(carry, inp):
        st, dec = inp
        return carry * dec[..., None, None] + st, carry

    final, prev = lax.scan(step, init_state.reshape(b, g, j, p, n),
                           (jnp.moveaxis(states, 1, 0), jnp.moveaxis(chunk_decay, 1, 0)))
    prev = jnp.moveaxis(prev, 0, 1)
    y_off = jnp.einsum('bclgn,bcgjpn,bcgjl->bclgjp', cc, prev, jnp.exp(a_cum))
    y = (y_diag + y_off).reshape(b, c * q, h, p)[:, :l]
    return y, final.reshape(b, h, p, n)


def ssd_conformer_mixer(h, conv_state, ccv_state, ssm_state, w_in, w_out, conv_w, conv_b,
                        dt_bias, a_log, d_skip, norm_g, ccv_w, ccv_b, ln_g, ln_b):
    b, l, _ = h.shape
    proj = h @ w_in
    z, xbc, dt_raw, u = jnp.split(
        proj, [SSM_D_INNER, SSM_D_INNER + SSM_CONV_DIM, SSM_D_INNER + SSM_CONV_DIM + SSM_HEADS], axis=-1)
    xbc_pad = jnp.concatenate([conv_state.astype(xbc.dtype), xbc], axis=1)
    new_conv = xbc_pad[:, xbc_pad.shape[1] - (SSM_CONV - 1):]
    xbc = jax.nn.silu(causal_dwconv(xbc_pad, conv_w, conv_b))
    xs, bm, cm = jnp.split(xbc, [SSM_D_INNER, SSM_D_INNER + SSM_GROUPS * SSM_STATE], axis=-1)
    xs4 = xs.reshape(b, l, SSM_HEADS, SSM_HEAD_DIM).astype(jnp.float32)
    dt = jax.nn.softplus(dt_raw.astype(jnp.float32) + dt_bias.astype(jnp.float32))
    a_neg = -jnp.exp(a_log.astype(jnp.float32))
    y, new_ssm = ssd_scan(xs4, dt, a_neg,
                          bm.reshape(b, l, SSM_GROUPS, SSM_STATE).astype(jnp.float32),
                          cm.reshape(b, l, SSM_GROUPS, SSM_STATE).astype(jnp.float32),
                          ssm_state.astype(jnp.float32))
    y = y + d_skip.astype(jnp.float32)[:, None] * xs4
    y = gated_group_rmsnorm(y.reshape(b, l, SSM_D_INNER).astype(h.dtype), z, norm_g)
    ua, ug = jnp.split(u, 2, axis=-1)
    glu = ua * jax.nn.sigmoid(ug)
    glu_pad = jnp.concatenate([ccv_state.astype(glu.dtype), glu], axis=1)
    new_ccv = glu_pad[:, glu_pad.shape[1] - (CCV_WIDTH - 1):]
    cv = jax.nn.silu(layernorm(causal_dwconv(glu_pad, ccv_w, ccv_b), ln_g, ln_b))
    out = jnp.concatenate([y, cv], axis=-1) @ w_out
    return out, new_conv, new_ccv, new_ssm.astype(ssm_state.dtype)


def fox_project(h, w_in, b_f):
    b, l, _ = h.shape
    proj = h @ w_in
    q, k, v, fl = jnp.split(proj, [FOX_WIDTH, 2 * FOX_WIDTH, 3 * FOX_WIDTH], axis=-1)
    shp = (b, l, FOX_HEADS, FOX_HEAD_DIM)
    log_f = jax.nn.log_sigmoid(fl.astype(jnp.float32) + b_f.astype(jnp.float32))
    return q.reshape(shp), k.reshape(shp), v.reshape(shp), log_f


def fox_attend_prompt(q, k, v, log_f):
    b, s = q.shape[:2]
    qb = min(Q_BLOCK, s)
    scale = FOX_HEAD_DIM ** -0.5
    fk = jnp.cumsum(log_f, axis=1).transpose(0, 2, 1)
    kpos = jnp.arange(s)

    def block(i):
        start = i * qb
        q_blk = lax.dynamic_slice_in_dim(q, start, qb, axis=1)
        fq = lax.dynamic_slice_in_dim(fk, start, qb, axis=2)
        sc = jnp.einsum('bqhd,bkhd->bhqk', q_blk, k).astype(jnp.float32) * scale
        sc = sc + (fq[..., :, None] - fk[:, :, None, :])
        qpos = start + jnp.arange(qb)
        sc = jnp.where(kpos[None, None, None, :] <= qpos[None, None, :, None], sc, -jnp.inf)
        pr = jax.nn.softmax(sc, axis=-1).astype(v.dtype)
        return jnp.einsum('bhqk,bkhd->bqhd', pr, v)

    o = lax.map(block, jnp.arange(s // qb))
    return jnp.moveaxis(o, 0, 1).reshape(b, s, FOX_HEADS, FOX_HEAD_DIM)


def fox_attend_paged(q, k, v, log_f, cache_k, cache_v, cache_logf, layer, page_table):
    past = page_table.shape[1] * PAGE_SIZE
    l = q.shape[1]
    scale = FOX_HEAD_DIM ** -0.5
    kpos = jnp.arange(past + l)
    qpos = past + jnp.arange(l)

    def one(args):
        qs, ks, vs, lfs, pages = args
        kp = cache_k[layer, pages].reshape(past, FOX_HEADS, FOX_HEAD_DIM).astype(ks.dtype)
        vp = cache_v[layer, pages].reshape(past, FOX_HEADS, FOX_HEAD_DIM).astype(vs.dtype)
        lfp = cache_logf[layer, pages].reshape(past, FOX_HEADS).astype(jnp.float32)
        k_all = jnp.concatenate([kp, ks], axis=0)
        v_all = jnp.concatenate([vp, vs], axis=0)
        f_all = jnp.cumsum(jnp.concatenate([lfp, lfs], axis=0), axis=0).T
        sc = jnp.einsum('qhd,khd->hqk', qs, k_all).astype(jnp.float32) * scale
        sc = sc + (f_all[:, past:, None] - f_all[:, None, :])
        sc = jnp.where(kpos[None, None, :] <= qpos[None, :, None], sc, -jnp.inf)
        pr = jax.nn.softmax(sc, axis=-1).astype(v_all.dtype)
        return jnp.einsum('hqk,khd->qhd', pr, v_all)

    return lax.map(one, (q, k, v, log_f, page_table))


def hier_moe(h, w_group, b_group, w_expert, b_expert, w_gate, w_up, w_down):
    b, l, d = h.shape
    t = h.reshape(-1, d)
    g_prob = jax.nn.softmax((t @ w_group).astype(jnp.float32) + b_group.astype(jnp.float32), axis=-1)
    g_val, g_idx = lax.top_k(g_prob, 1)
    e_logits = ((t @ w_expert).astype(jnp.float32) + b_expert.astype(jnp.float32)).reshape(
        -1, MOE_GROUPS, MOE_EXPERTS_PER_GROUP)
    e_in = jnp.take_along_axis(e_logits, g_idx[:, :, None], axis=1)[:, 0]
    e_val, e_idx = lax.top_k(e_in, MOE_TOP_K)
    e_w = jax.nn.softmax(e_val, axis=-1) * g_val
    e_id = g_idx * MOE_EXPERTS_PER_GROUP + e_idx
    combine = jnp.sum(jax.nn.one_hot(e_id, MOE_EXPERTS, dtype=jnp.float32) * e_w[..., None], axis=1)
    hg = jnp.einsum('td,edf->tef', t, w_gate)
    hu = jnp.einsum('td,edf->tef', t, w_up)
    act = jax.nn.silu(hg) * hu * combine[..., None].astype(t.dtype)
    out = jnp.einsum('tef,efd->td', act, w_down)
    return out.reshape(b, l, d)


def setup_inputs(seed: int = 0) -> dict:
    key = jax.random.key(seed)
    ks = iter(jax.random.split(key, 48))
    nrm = lambda shape, s: jax.random.normal(next(ks), shape, jnp.float32) * s
    uni = lambda shape, lo, hi: jax.random.uniform(next(ks), shape, jnp.float32, lo, hi)
    n_pages = PAST_LEN // PAGE_SIZE
    n_used = DEC_BATCH * n_pages
    n_pool = n_used + n_used // 4
    d = D_MODEL
    x_prompt = nrm((BATCH, SEQ, d), 1.0)
    x_sample = nrm((DEC_BATCH, DEC_SEQ, d), 1.0)
    cache_k = nrm((N_ATT_LAYERS, n_pool, PAGE_SIZE, FOX_HEADS, FOX_HEAD_DIM), 1.0)
    cache_v = nrm((N_ATT_LAYERS, n_pool, PAGE_SIZE, FOX_HEADS, FOX_HEAD_DIM), 1.0)
    cache_logf = jax.nn.log_sigmoid(uni((N_ATT_LAYERS, n_pool, PAGE_SIZE, FOX_HEADS), 1.0, 6.0))
    state_ssm = nrm((N_AB_LAYERS, DEC_BATCH, SSM_HEADS, SSM_HEAD_DIM, SSM_STATE), 0.3)
    state_ssm_conv = nrm((N_AB_LAYERS, DEC_BATCH, SSM_CONV - 1, SSM_CONV_DIM), 1.0)
    state_ccv = nrm((N_AB_LAYERS, DEC_BATCH, CCV_WIDTH - 1, CCV_DIM), 0.5)
    page_table = jax.random.permutation(next(ks), n_pool)[:n_used].reshape(
        DEC_BATCH, n_pages).astype(jnp.int32)
    c_prompt = nrm((BATCH, d), 1.0)
    c_sample = nrm((DEC_BATCH, d), 1.0)
    ada_w = nrm((DEPTH, 2, d, 3 * d), 0.5 * d ** -0.5)
    ada_b = nrm((DEPTH, 2, 3 * d), 0.02)
    norm_g = 1.0 + nrm((DEPTH, 2, d), 0.05)
    ab_w_in = nrm((N_AB_LAYERS, d, AB_IN), d ** -0.5)
    ssm_conv_w = nrm((N_AB_LAYERS, SSM_CONV, SSM_CONV_DIM), SSM_CONV ** -0.5)
    ssm_conv_b = nrm((N_AB_LAYERS, SSM_CONV_DIM), 0.02)
    dt0 = jnp.exp(uni((N_AB_LAYERS, SSM_HEADS), math.log(1e-3), math.log(1e-1)))
    ssm_dt_bias = dt0 + jnp.log(-jnp.expm1(-dt0))
    ssm_a_log = jnp.log(uni((N_AB_LAYERS, SSM_HEADS), 1.0, 16.0))
    ssm_d = 1.0 + nrm((N_AB_LAYERS, SSM_HEADS), 0.1)
    ssm_norm_g = 1.0 + nrm((N_AB_LAYERS, SSM_D_INNER), 0.05)
    ccv_w = nrm((N_AB_LAYERS, CCV_WIDTH, CCV_DIM), CCV_WIDTH ** -0.5)
    ccv_b = nrm((N_AB_LAYERS, CCV_DIM), 0.02)
    ccv_ln_g = 1.0 + nrm((N_AB_LAYERS, CCV_DIM), 0.05)
    ccv_ln_b = nrm((N_AB_LAYERS, CCV_DIM), 0.02)
    ab_w_out = nrm((N_AB_LAYERS, AB_MIX, d), AB_MIX ** -0.5)
    fox_w_in = nrm((N_ATT_LAYERS, d, FOX_IN), d ** -0.5)
    fox_b_f = uni((N_ATT_LAYERS, FOX_HEADS), 1.0, 6.0)
    fox_w_out = nrm((N_ATT_LAYERS, FOX_WIDTH, d), FOX_WIDTH ** -0.5)
    moe_w_group = nrm((DEPTH, d, MOE_GROUPS), d ** -0.5)
    moe_b_group = nrm((DEPTH, MOE_GROUPS), 0.01)
    moe_w_expert = nrm((DEPTH, d, MOE_EXPERTS), d ** -0.5)
    moe_b_expert = nrm((DEPTH, MOE_EXPERTS), 0.01)
    moe_w_gate = nrm((DEPTH, MOE_EXPERTS, d, MOE_FF), d ** -0.5)
    moe_w_up = nrm((DEPTH, MOE_EXPERTS, d, MOE_FF), d ** -0.5)
    moe_w_down = nrm((DEPTH, MOE_EXPERTS, MOE_FF, d), MOE_FF ** -0.5)
    final_g = 1.0 + nrm((d,), 0.05)
    return {
        "x_prompt": x_prompt, "x_sample": x_sample,
        "cache_k": cache_k, "cache_v": cache_v, "cache_logf": cache_logf,
        "state_ssm": state_ssm, "state_ssm_conv": state_ssm_conv, "state_ccv": state_ccv,
        "page_table": page_table, "c_prompt": c_prompt, "c_sample": c_sample,
        "ada_w": ada_w, "ada_b": ada_b, "norm_g": norm_g,
        "ab_w_in": ab_w_in, "ssm_conv_w": ssm_conv_w, "ssm_conv_b": ssm_conv_b,
        "ssm_dt_bias": ssm_dt_bias, "ssm_a_log": ssm_a_log, "ssm_d": ssm_d, "ssm_norm_g": ssm_norm_g,
        "ccv_w": ccv_w, "ccv_b": ccv_b, "ccv_ln_g": ccv_ln_g, "ccv_ln_b": ccv_ln_b,
        "ab_w_out": ab_w_out, "fox_w_in": fox_w_in, "fox_b_f": fox_b_f, "fox_w_out": fox_w_out,
        "moe_w_group": moe_w_group, "moe_b_group": moe_b_group,
        "moe_w_expert": moe_w_expert, "moe_b_expert": moe_b_expert,
        "moe_w_gate": moe_w_gate, "moe_w_up": moe_w_up, "moe_w_down": moe_w_down,
        "final_g": final_g,
    }


def reference(x_prompt, x_sample, cache_k, cache_v, cache_logf, state_ssm, state_ssm_conv, state_ccv,
              page_table, c_prompt, c_sample, ada_w, ada_b, norm_g, ab_w_in, ssm_conv_w, ssm_conv_b,
              ssm_dt_bias, ssm_a_log, ssm_d, ssm_norm_g, ccv_w, ccv_b, ccv_ln_g, ccv_ln_b, ab_w_out,
              fox_w_in, fox_b_f, fox_w_out, moe_w_group, moe_b_group, moe_w_expert, moe_b_expert,
              moe_w_gate, moe_w_up, moe_w_down, final_g):
    yp, ys = x_prompt, x_sample
    bp = x_prompt.shape[0]
    kp_l, vp_l, fp_l, ks_l, vs_l, fs_l = [], [], [], [], [], []
    sp_l, ss_l, cp_l, cs_l, gp_l, gs_l = [], [], [], [], [], []
    for layer in range(DEPTH):
        i = layer // 2
        sh_p, sc_p, gt_p = ada_params(c_prompt, ada_w[layer, 0], ada_b[layer, 0])
        sh_s, sc_s, gt_s = ada_params(c_sample, ada_w[layer, 0], ada_b[layer, 0])
        hp = modulate(yp, norm_g[layer, 0], sh_p, sc_p)
        hs = modulate(ys, norm_g[layer, 0], sh_s, sc_s)
        if layer % 2 == 0:
            w = (ab_w_in[i], ab_w_out[i], ssm_conv_w[i], ssm_conv_b[i], ssm_dt_bias[i], ssm_a_log[i],
                 ssm_d[i], ssm_norm_g[i], ccv_w[i], ccv_b[i], ccv_ln_g[i], ccv_ln_b[i])
            op, cv_p, cc_p, st_p = ssd_conformer_mixer(
                hp, jnp.zeros((bp, SSM_CONV - 1, SSM_CONV_DIM), hp.dtype),
                jnp.zeros((bp, CCV_WIDTH - 1, CCV_DIM), hp.dtype),
                jnp.zeros((bp, SSM_HEADS, SSM_HEAD_DIM, SSM_STATE), hp.dtype), *w)
            os_, cv_s, cc_s, st_s = ssd_conformer_mixer(
                hs, state_ssm_conv[i], state_ccv[i], state_ssm[i], *w)
            sp_l.append(st_p); ss_l.append(st_s)
            cp_l.append(cv_p); cs_l.append(cv_s)
            gp_l.append(cc_p); gs_l.append(cc_s)
        else:
            qp, kp, vp, lfp = fox_project(hp, fox_w_in[i], fox_b_f[i])
            op = fox_attend_prompt(qp, kp, vp, lfp).reshape(hp.shape[0], hp.shape[1], FOX_WIDTH) @ fox_w_out[i]
            qs, kn, vn, lfs = fox_project(hs, fox_w_in[i], fox_b_f[i])
            os_ = fox_attend_paged(qs, kn, vn, lfs, cache_k, cache_v, cache_logf, i, page_table).reshape(
                hs.shape[0], hs.shape[1], FOX_WIDTH) @ fox_w_out[i]
            kp_l.append(kp); vp_l.append(vp); fp_l.append(lfp.astype(cache_logf.dtype))
            ks_l.append(kn); vs_l.append(vn); fs_l.append(lfs.astype(cache_logf.dtype))
        yp = yp + gt_p * op
        ys = ys + gt_s * os_
        sh_p, sc_p, gt_p = ada_params(c_prompt, ada_w[layer, 1], ada_b[layer, 1])
        sh_s, sc_s, gt_s = ada_params(c_sample, ada_w[layer, 1], ada_b[layer, 1])
        mw = (moe_w_group[layer], moe_b_group[layer], moe_w_expert[layer], moe_b_expert[layer],
              moe_w_gate[layer], moe_w_up[layer], moe_w_down[layer])
        yp = yp + gt_p * hier_moe(modulate(yp, norm_g[layer, 1], sh_p, sc_p), *mw)
        ys = ys + gt_s * hier_moe(modulate(ys, norm_g[layer, 1], sh_s, sc_s), *mw)
    y_prompt = rmsnorm(yp, final_g)
    y_sample = rmsnorm(ys, final_g)
    new_k_prompt = jnp.stack(kp_l, 0)
    new_v_prompt = jnp.stack(vp_l, 0)
    new_logf_prompt = jnp.stack(fp_l, 0)
    new_k_sample = jnp.stack(ks_l, 0)
    new_v_sample = jnp.stack(vs_l, 0)
    new_logf_sample = jnp.stack(fs_l, 0)
    ssm_state_prompt = jnp.stack(sp_l, 0)
    ssm_state_sample = jnp.stack(ss_l, 0)
    ssm_conv_prompt = jnp.stack(cp_l, 0)
    ssm_conv_sample = jnp.stack(cs_l, 0)
    ccv_prompt = jnp.stack(gp_l, 0)
    ccv_sample = jnp.stack(gs_l, 0)
    return (y_prompt, y_sample, new_k_prompt, new_v_prompt, new_logf_prompt, new_k_sample, new_v_sample,
            new_logf_sample, ssm_state_prompt, ssm_state_sample, ssm_conv_prompt, ssm_conv_sample,
            ccv_prompt, ccv_sample)
```

```python
import functools

import jax
import jax.numpy as jnp
from jax import lax
from jax.experimental import pallas as pl
from jax.experimental.pallas import tpu as pltpu

F32 = jnp.float32
BF16 = jnp.bfloat16
HIGHEST = lax.Precision.HIGHEST

LANES = 128
SUBLANES = 8
VMEM_LIMIT_BYTES = 56 * 1024 * 1024

EPS = 1e-6
D_MODEL = 2048
SSM_HEAD_DIM = 64
SSM_HEADS = 32
SSM_GROUPS = 4
SSM_STATE = 128
SSM_CONV = 4
SSM_CHUNK = 128
SSM_D_INNER = SSM_HEADS * SSM_HEAD_DIM
SSM_BC = SSM_GROUPS * SSM_STATE
SSM_CONV_DIM = SSM_D_INNER + 2 * SSM_BC
SSM_PAIRS = SSM_HEADS // 2
PAIRS_PER_GROUP = SSM_PAIRS // SSM_GROUPS
CCV_DIM = 1024
CCV_WIDTH = 31
CCV_HALO = 32
FOX_HEADS = 16
FOX_HEAD_DIM = 128
FOX_WIDTH = FOX_HEADS * FOX_HEAD_DIM
PAGE_SIZE = 128
MOE_GROUPS = 4
MOE_EPG = 4
MOE_EXPERTS = 16
MOE_FF = 512
NEG_INF = float("-inf")


def _params(*sem):
    return pltpu.CompilerParams(dimension_semantics=sem, vmem_limit_bytes=VMEM_LIMIT_BYTES)


def _silu(x):
    return x * jax.nn.sigmoid(x)


def _softplus(x):
    return jnp.maximum(x, 0.0) + jnp.log1p(jnp.exp(-jnp.abs(x)))


def _dot(a, b):
    return jnp.dot(a, b, preferred_element_type=F32)


def _dot_nt(a, b):
    return lax.dot_general(a, b, (((1,), (1,)), ((), ())), preferred_element_type=F32)


def _dot_exact(a, b):
    return jnp.dot(a, b, precision=HIGHEST, preferred_element_type=F32)


def _split3(x):
    hi = x.astype(BF16)
    r1 = x - hi.astype(F32)
    mid = r1.astype(BF16)
    lo = (r1 - mid.astype(F32)).astype(BF16)
    return hi, mid, lo


def _mask_dot(mask, x):
    mb = mask.astype(F32).astype(BF16)
    hi, mid, lo = _split3(x)
    return _dot(mb, hi) + _dot(mb, mid) + _dot(mb, lo)


def _dot_mask(x, mask):
    mb = mask.astype(F32).astype(BF16)
    hi, mid, lo = _split3(x)
    return _dot(hi, mb) + _dot(mid, mb) + _dot(lo, mb)


def _row_tile(n, pref):
    t = min(n, pref)
    while n % t:
        t //= 2
    return t


def _ada_kernel(c_ref, w_ref, b_ref, o_ref):
    sc = _silu(c_ref[...]).astype(BF16)
    o_ref[0] = _dot(sc, w_ref[0].astype(BF16)) + b_ref[0]


def ada_all(c_all, w, b):
    r, d = c_all.shape
    s, _, n = w.shape
    tn = 1024
    return pl.pallas_call(
        _ada_kernel,
        grid=(s, n // tn),
        in_specs=[pl.BlockSpec((r, d), lambda i, j: (0, 0)),
                  pl.BlockSpec((1, d, tn), lambda i, j: (i, 0, j)),
                  pl.BlockSpec((1, 1, tn), lambda i, j: (i, 0, j))],
        out_specs=pl.BlockSpec((1, r, tn), lambda i, j: (i, 0, j)),
        out_shape=jax.ShapeDtypeStruct((s, r, n), F32),
        compiler_params=_params("arbitrary", "arbitrary"),
        name="ada_all",
    )(c_all, w, b)


def _mod_spec(mod, tm, tn=None):
    if tn is None:
        d = mod.shape[-1]
        if mod.shape[1] == 1:
            return pl.BlockSpec((1, 1, d), lambda b, i, j: (b, 0, 0))
        return pl.BlockSpec((1, tm, d), lambda b, i, j: (b, i, 0))
    if mod.shape[1] == 1:
        return pl.BlockSpec((1, 1, tn), lambda b, i, j: (b, 0, j))
    return pl.BlockSpec((1, tm, tn), lambda b, i, j: (b, i, j))


def _norm_mod(x, g, shift, scale):
    ms = jnp.mean(x * x, axis=-1, keepdims=True)
    return (x * lax.rsqrt(ms + EPS) * g) * (1.0 + scale) + shift


def _nm_matmul_kernel(x_ref, g_ref, sh_ref, sc_ref, w_ref, *rest, ranges, scales, has_aux):
    n_out = len(ranges)
    if has_aux:
        waux_ref = rest[0]
        outs = rest[1:1 + n_out]
        aux_ref = rest[1 + n_out]
    else:
        outs = rest[:n_out]
    h_scr = rest[-1]
    j = pl.program_id(2)

    @pl.when(j == 0)
    def _():
        h = _norm_mod(x_ref[0], g_ref[...], sh_ref[0], sc_ref[0])
        h_scr[...] = h.astype(BF16)
        if has_aux:
            aux_ref[0] = _dot(h_scr[...], waux_ref[...])

    res = _dot(h_scr[...], w_ref[...])
    for (lo, hi), scale, o_ref in zip(ranges, scales, outs):
        @pl.when((j >= lo) & (j < hi))
        def _(o_ref=o_ref, scale=scale):
            val = res if scale is None else res * scale
            o_ref[0] = val.astype(o_ref.dtype)


def nm_matmul(x, g, shift, scale, w, outs, w_aux=None, tm_pref=512, tn=512):
    bsz, l, d = x.shape
    n = w.shape[1]
    tm = _row_tile(l, tm_pref)
    ranges, scales, out_shapes, out_specs = [], [], [], []
    lo = 0
    for width, dtype, sc in outs:
        nt = width // tn
        ranges.append((lo, lo + nt))
        scales.append(sc)
        out_shapes.append(jax.ShapeDtypeStruct((bsz, l, width), dtype))
        out_specs.append(pl.BlockSpec(
            (1, tm, tn), lambda b, i, j, lo=lo, nt=nt: (b, i, jnp.clip(j - lo, 0, nt - 1))))
        lo += nt
    assert lo * tn == n
    in_specs = [pl.BlockSpec((1, tm, d), lambda b, i, j: (b, i, 0)),
                pl.BlockSpec((1, d), lambda b, i, j: (0, 0)),
                _mod_spec(shift, tm), _mod_spec(scale, tm),
                pl.BlockSpec((d, tn), lambda b, i, j: (0, j))]
    args = [x, g, shift, scale, w]
    if w_aux is not None:
        in_specs.append(pl.BlockSpec((d, LANES), lambda b, i, j: (0, 0)))
        args.append(w_aux)
        out_shapes.append(jax.ShapeDtypeStruct((bsz, l, LANES), F32))
        out_specs.append(pl.BlockSpec((1, tm, LANES), lambda b, i, j: (b, i, 0)))
    return pl.pallas_call(
        functools.partial(_nm_matmul_kernel, ranges=tuple(ranges), scales=tuple(scales),
                          has_aux=w_aux is not None),
        grid=(bsz, l // tm, n // tn),
        in_specs=in_specs, out_specs=out_specs, out_shape=out_shapes,
        scratch_shapes=[pltpu.VMEM((tm, d), BF16)],
        compiler_params=_params("arbitrary", "arbitrary", "arbitrary"),
        name="nm_matmul",
    )(*args)


def _ssd_kernel(xbc_ref, z_ref, dt_ref, cw_ref, cb_ref, dtb_ref, alog_ref, dsk_ref, ng_ref,
                ic_ref, is_ref, y_ref, fs_ref, ext_scr, xc_scr, st_scr, y_scr, *, q, l_valid):
    c = pl.program_id(1)
    nc = pl.num_programs(1)

    @pl.when(c == 0)
    def _():
        ext_scr[0:SUBLANES, :] = ic_ref[0]
        st_scr[...] = is_ref[0]

    @pl.when(c > 0)
    def _():
        ext_scr[0:SUBLANES, :] = ext_scr[q:q + SUBLANES, :]

    ext_scr[SUBLANES:SUBLANES + q, :] = xbc_ref[0].astype(F32)
    conv = cb_ref[...]
    for k in range(SSM_CONV):
        off = SUBLANES - (SSM_CONV - 1) + k
        conv = conv + cw_ref[k:k + 1, :] * ext_scr[off:off + q, :]
    xc_scr[...] = _silu(conv)

    row = lax.broadcasted_iota(jnp.int32, (q, LANES), 0)
    lane = lax.broadcasted_iota(jnp.int32, (q, LANES), 1)
    left = lane < SSM_HEAD_DIM
    dt = _softplus(dt_ref[0] + dtb_ref[...])
    dt = jnp.where(row + c * q < l_valid, dt, 0.0)
    a = dt * (-jnp.exp(alog_ref[...]))
    rq = lax.broadcasted_iota(jnp.int32, (q, q), 0)
    cq = lax.broadcasted_iota(jnp.int32, (q, q), 1)
    causal = rq >= cq
    a_cum = _mask_dot(causal, a)
    a_cum_t = a_cum.T
    a_last = a_cum[q - 1:q, :]

    for g in range(SSM_GROUPS):
        b_lo = SSM_D_INNER + g * SSM_STATE
        c_lo = SSM_D_INNER + SSM_BC + g * SSM_STATE
        bm = xc_scr[:, b_lo:b_lo + SSM_STATE].astype(BF16)
        cm = xc_scr[:, c_lo:c_lo + SSM_STATE].astype(BF16)
        cb = _dot_nt(cm, bm)
        for pp in range(PAIRS_PER_GROUP):
            p = g * PAIRS_PER_GROUP + pp
            h0, h1 = 2 * p, 2 * p + 1
            col0, col1 = a_cum[:, h0:h0 + 1], a_cum[:, h1:h1 + 1]
            l0 = jnp.exp(jnp.where(causal, col0 - a_cum_t[h0:h0 + 1, :], NEG_INF))
            l1 = jnp.exp(jnp.where(causal, col1 - a_cum_t[h1:h1 + 1, :], NEG_INF))
            m = jnp.concatenate([cb * l0, cb * l1], axis=1).astype(BF16)
            xs = xc_scr[:, p * LANES:(p + 1) * LANES]
            xd = xs * jnp.where(left, dt[:, h0:h0 + 1], dt[:, h1:h1 + 1])
            xd_bd = jnp.concatenate([jnp.where(left, xd, 0.0), jnp.where(left, 0.0, xd)],
                                    axis=0).astype(BF16)
            y = _dot(m, xd_bd)
            st = st_scr[p]
            y = y + _dot_nt(cm, st.astype(BF16)) * jnp.exp(jnp.where(left, col0, col1))
            y_scr[:, p * LANES:(p + 1) * LANES] = y + dsk_ref[:, p * LANES:(p + 1) * LANES] * xs
            al0, al1 = a_last[:, h0:h0 + 1], a_last[:, h1:h1 + 1]
            dte = jnp.exp(jnp.where(left, al0 - col0, al1 - col1))
            upd = _dot((xd * dte).T.astype(BF16), bm)
            half = (SSM_HEAD_DIM, SSM_STATE)
            decay = jnp.exp(jnp.concatenate([jnp.broadcast_to(al0, half), jnp.broadcast_to(al1, half)],
                                            axis=0))
            st_scr[p] = st * decay + upd

    gw = SSM_D_INNER // SSM_GROUPS
    for g in range(SSM_GROUPS):
        sl = slice(g * gw, (g + 1) * gw)
        v = y_scr[:, sl] * _silu(z_ref[0, :, sl].astype(F32))
        v = v * lax.rsqrt(jnp.mean(v * v, axis=-1, keepdims=True) + EPS)
        y_ref[0, :, sl] = (v * ng_ref[:, sl]).astype(y_ref.dtype)

    @pl.when(c == nc - 1)
    def _():
        fs_ref[0] = st_scr[...]


def ssd_mixer(xbc, z, dt_raw, conv_w, conv_b, dt_bias, a_log, d_skip, norm_g, init_conv, init_state,
              l_valid, out_dtype):
    bsz, l, _ = xbc.shape
    q = SSM_CHUNK
    assert l % q == 0
    vec = lambda n: pl.BlockSpec((1, n), lambda b, c: (0, 0))
    state_spec = pl.BlockSpec((1, SSM_PAIRS, LANES, SSM_STATE), lambda b, c: (b, 0, 0, 0))
    return pl.pallas_call(
        functools.partial(_ssd_kernel, q=q, l_valid=l_valid),
        grid=(bsz, l // q),
        in_specs=[pl.BlockSpec((1, q, SSM_CONV_DIM), lambda b, c: (b, c, 0)),
                  pl.BlockSpec((1, q, SSM_D_INNER), lambda b, c: (b, c, 0)),
                  pl.BlockSpec((1, q, LANES), lambda b, c: (b, c, 0)),
                  pl.BlockSpec((SSM_CONV, SSM_CONV_DIM), lambda b, c: (0, 0)),
                  vec(SSM_CONV_DIM), vec(LANES), vec(LANES), vec(SSM_D_INNER), vec(SSM_D_INNER),
                  pl.BlockSpec((1, SUBLANES, SSM_CONV_DIM), lambda b, c: (b, 0, 0)),
                  state_spec],
        out_specs=[pl.BlockSpec((1, q, SSM_D_INNER), lambda b, c: (b, c, 0)), state_spec],
        out_shape=[jax.ShapeDtypeStruct((bsz, l, SSM_D_INNER), out_dtype),
                   jax.ShapeDtypeStruct((bsz, SSM_PAIRS, LANES, SSM_STATE), F32)],
        scratch_shapes=[pltpu.VMEM((q + SUBLANES, SSM_CONV_DIM), F32),
                        pltpu.VMEM((q, SSM_CONV_DIM), F32),
                        pltpu.VMEM((SSM_PAIRS, LANES, SSM_STATE), F32),
                        pltpu.VMEM((q, SSM_D_INNER), F32)],
        compiler_params=_params("arbitrary", "arbitrary"),
        name="ssd_mixer",
    )(xbc, z, dt_raw, conv_w, conv_b, dt_bias, a_log, d_skip, norm_g, init_conv, init_state)


def _ccv_kernel(u_ref, w_ref, b_ref, lg_ref, lb_ref, init_ref, o_ref, new_ref, ext_scr, cv_scr,
                win_scr, *, tl, lv_last, rb):
    t = pl.program_id(1)
    nt = pl.num_programs(1)

    @pl.when(t == 0)
    def _():
        ext_scr[0:CCV_HALO, :] = init_ref[0]

    @pl.when(t > 0)
    def _():
        ext_scr[0:CCV_HALO, :] = ext_scr[tl:tl + CCV_HALO, :]

    ua = u_ref[0, :, 0:CCV_DIM].astype(F32)
    ug = u_ref[0, :, CCV_DIM:2 * CCV_DIM].astype(F32)
    ext_scr[CCV_HALO:CCV_HALO + tl, :] = ua * jax.nn.sigmoid(ug)

    first = CCV_HALO - (CCV_WIDTH - 1)

    def body(r, carry):
        r0 = pl.multiple_of(r * rb, rb)
        win_scr[...] = ext_scr[pl.ds(r0, rb + CCV_HALO), :]
        acc = jnp.broadcast_to(b_ref[...], (rb, CCV_DIM))
        for k in range(CCV_WIDTH):
            acc = acc + w_ref[k:k + 1, :] * win_scr[first + k:first + k + rb, :]
        cv_scr[pl.ds(r0, rb), :] = acc
        return carry

    lax.fori_loop(0, tl // rb, body, 0)
    cv = cv_scr[...]
    mu = jnp.mean(cv, axis=-1, keepdims=True)
    var = jnp.mean(jnp.square(cv - mu), axis=-1, keepdims=True)
    y = (cv - mu) * lax.rsqrt(var + EPS) * lg_ref[...] + lb_ref[...]
    o_ref[0] = _silu(y).astype(o_ref.dtype)

    @pl.when(t == nt - 1)
    def _():
        new_ref[0] = ext_scr[lv_last:lv_last + CCV_HALO, :]


def ccv_mixer(u, w, b, ln_g, ln_b, init, l_valid, out_dtype):
    bsz, l, _ = u.shape
    tl = _row_tile(l, 512)
    lv_last = l_valid - (l - tl)
    rb = min(tl, 32)
    assert 0 < lv_last <= tl and (tl >= CCV_HALO or l == tl)
    vec = lambda: pl.BlockSpec((1, CCV_DIM), lambda bb, t: (0, 0))
    return pl.pallas_call(
        functools.partial(_ccv_kernel, tl=tl, lv_last=lv_last, rb=rb),
        grid=(bsz, l // tl),
        in_specs=[pl.BlockSpec((1, tl, 2 * CCV_DIM), lambda bb, t: (bb, t, 0)),
                  pl.BlockSpec((CCV_WIDTH, CCV_DIM), lambda bb, t: (0, 0)),
                  vec(), vec(), vec(),
                  pl.BlockSpec((1, CCV_HALO, CCV_DIM), lambda bb, t: (bb, 0, 0))],
        out_specs=[pl.BlockSpec((1, tl, CCV_DIM), lambda bb, t: (bb, t, 0)),
                   pl.BlockSpec((1, CCV_HALO, CCV_DIM), lambda bb, t: (bb, 0, 0))],
        out_shape=[jax.ShapeDtypeStruct((bsz, l, CCV_DIM), out_dtype),
                   jax.ShapeDtypeStruct((bsz, CCV_HALO, CCV_DIM), F32)],
        scratch_shapes=[pltpu.VMEM((tl + 2 * CCV_HALO, CCV_DIM), F32),
                        pltpu.VMEM((tl, CCV_DIM), F32),
                        pltpu.VMEM((rb + CCV_HALO, CCV_DIM), F32)],
        compiler_params=_params("arbitrary", "arbitrary"),
        name="ccv_mixer",
    )(u, w, b, ln_g, ln_b, init)


def _mm_res_kernel(*refs, n_lhs):
    a_refs = refs[:n_lhs]
    w_refs = refs[n_lhs:2 * n_lhs]
    res_ref, gate_ref, o_ref = refs[2 * n_lhs:]
    acc = _dot(a_refs[0][0].astype(BF16), w_refs[0][...])
    for a_ref, w_ref in zip(a_refs[1:], w_refs[1:]):
        acc = acc + _dot(a_ref[0].astype(BF16), w_ref[...])
    o_ref[0] = res_ref[0] + gate_ref[0] * acc


def mm_residual(lhs, w, res, gate, tm_pref=1024, tn=512):
    bsz, l, n = res.shape
    tm = _row_tile(l, tm_pref)
    in_specs, w_specs = [], []
    row = 0
    for a in lhs:
        k = a.shape[-1]
        assert row % k == 0
        in_specs.append(pl.BlockSpec((1, tm, k), lambda b, i, j: (b, i, 0)))
        w_specs.append(pl.BlockSpec((k, tn), lambda b, i, j, rb=row // k: (rb, j)))
        row += k
    assert row == w.shape[0]
    in_specs += w_specs + [pl.BlockSpec((1, tm, tn), lambda b, i, j: (b, i, j)), _mod_spec(gate, tm, tn)]
    return pl.pallas_call(
        functools.partial(_mm_res_kernel, n_lhs=len(lhs)),
        grid=(bsz, l // tm, n // tn),
        in_specs=in_specs,
        out_specs=pl.BlockSpec((1, tm, tn), lambda b, i, j: (b, i, j)),
        out_shape=jax.ShapeDtypeStruct((bsz, l, n), F32),
        compiler_params=_params("arbitrary", "arbitrary", "arbitrary"),
        name="mm_residual",
    )(*lhs, *([w] * len(lhs)), res, gate)


def _router_kernel(x_ref, g_ref, sh_ref, sc_ref, wr_ref, br_ref, h_ref, comb_ref):
    h = _norm_mod(x_ref[0], g_ref[...], sh_ref[0], sc_ref[0])
    h_ref[0] = h.astype(h_ref.dtype)
    logits = _dot_exact(h, wr_ref[...]) + br_ref[...]
    tm = logits.shape[0]
    lane = lax.broadcasted_iota(jnp.int32, (tm, LANES), 1).astype(F32)
    first_of = lambda hit: jnp.min(jnp.where(hit, lane, float(LANES)), axis=-1, keepdims=True)
    is_group = (lane >= MOE_EXPERTS) & (lane < MOE_EXPERTS + MOE_GROUPS)
    gl = jnp.where(is_group, logits, NEG_INF)
    gmax = jnp.max(gl, axis=-1, keepdims=True)
    gsum = jnp.sum(jnp.exp(gl - gmax), axis=-1, keepdims=True)
    g_val = 1.0 / gsum
    g_idx = first_of(gl == gmax) - MOE_EXPERTS
    in_group = (lane >= g_idx * MOE_EPG) & (lane < (g_idx + 1.0) * MOE_EPG)
    e1 = jnp.where(in_group, logits, NEG_INF)
    m1 = jnp.max(e1, axis=-1, keepdims=True)
    i1 = first_of(e1 == m1)
    e2 = jnp.where(lane == i1, NEG_INF, e1)
    m2 = jnp.max(e2, axis=-1, keepdims=True)
    i2 = first_of(e2 == m2)
    r = jnp.exp(m2 - m1)
    w1 = g_val / (1.0 + r)
    w2 = g_val * r / (1.0 + r)
    comb_ref[0] = jnp.where(lane == i1, w1, 0.0) + jnp.where(lane == i2, w2, 0.0)


def moe_router(x, g, shift, scale, w_router, b_router, tm_pref=512):
    bsz, l, d = x.shape
    tm = _row_tile(l, tm_pref)
    return pl.pallas_call(
        _router_kernel,
        grid=(bsz, l // tm, 1),
        in_specs=[pl.BlockSpec((1, tm, d), lambda b, i, j: (b, i, 0)),
                  pl.BlockSpec((1, d), lambda b, i, j: (0, 0)),
                  _mod_spec(shift, tm), _mod_spec(scale, tm),
                  pl.BlockSpec((d, LANES), lambda b, i, j: (0, 0)),
                  pl.BlockSpec((1, LANES), lambda b, i, j: (0, 0))],
        out_specs=[pl.BlockSpec((1, tm, d), lambda b, i, j: (b, i, 0)),
                   pl.BlockSpec((1, tm, LANES), lambda b, i, j: (b, i, 0))],
        out_shape=[jax.ShapeDtypeStruct((bsz, l, d), BF16),
                   jax.ShapeDtypeStruct((bsz, l, LANES), F32)],
        compiler_params=_params("arbitrary", "arbitrary", "arbitrary"),
        name="moe_router",
    )(x, g, shift, scale, w_router, b_router)


def _moe_kernel(h_ref, comb_ref, wgu_ref, wd_ref, res_ref, gate_ref, o_ref, acc_scr):
    e = pl.program_id(2)

    @pl.when(e == 0)
    def _():
        acc_scr[...] = jnp.zeros_like(acc_scr)

    comb = comb_ref[0]
    lane = lax.broadcasted_iota(jnp.int32, comb.shape, 1)
    col = jnp.sum(jnp.where(lane == e, comb, 0.0), axis=-1, keepdims=True)

    @pl.when(jnp.max(jnp.abs(col)) > 0.0)
    def _():
        gu = _dot(h_ref[0], wgu_ref[0])
        act = _silu(gu[:, :MOE_FF]) * gu[:, MOE_FF:] * col
        acc_scr[...] += _dot(act.astype(BF16), wd_ref[0])

    @pl.when(e == pl.num_programs(2) - 1)
    def _():
        o_ref[0] = res_ref[0] + gate_ref[0] * acc_scr[...]


def moe_experts(h, comb, w_gu, w_d, res, gate, tm_pref=512):
    bsz, l, d = h.shape
    tm = _row_tile(l, tm_pref)
    e = w_gu.shape[0]
    if gate.shape[1] == 1:
        gate_spec = pl.BlockSpec((1, 1, d), lambda b, i, j: (b, 0, 0))
    else:
        gate_spec = pl.BlockSpec((1, tm, d), lambda b, i, j: (b, i, 0))
    return pl.pallas_call(
        _moe_kernel,
        grid=(bsz, l // tm, e),
        in_specs=[pl.BlockSpec((1, tm, d), lambda b, i, j: (b, i, 0)),
                  pl.BlockSpec((1, tm, LANES), lambda b, i, j: (b, i, 0)),
                  pl.BlockSpec((1, d, 2 * MOE_FF), lambda b, i, j: (j, 0, 0)),
                  pl.BlockSpec((1, MOE_FF, d), lambda b, i, j: (j, 0, 0)),
                  pl.BlockSpec((1, tm, d), lambda b, i, j: (b, i, 0)),
                  gate_spec],
        out_specs=pl.BlockSpec((1, tm, d), lambda b, i, j: (b, i, 0)),
        out_shape=jax.ShapeDtypeStruct((bsz, l, d), F32),
        scratch_shapes=[pltpu.VMEM((tm, d), F32)],
        compiler_params=_params("arbitrary", "arbitrary", "arbitrary"),
        name="moe_experts",
    )(h, comb, w_gu, w_d, res, gate)


def _logf_kernel(fl_ref, bf_ref, lf_ref, cum_ref, cumt_ref, carry_scr, *, tc):
    c = pl.program_id(1)

    @pl.when(c == 0)
    def _():
        carry_scr[...] = jnp.zeros_like(carry_scr)

    x = fl_ref[0] + bf_ref[...]
    lf = jnp.minimum(x, 0.0) - jnp.log1p(jnp.exp(-jnp.abs(x)))
    lf_ref[0] = lf
    r = lax.broadcasted_iota(jnp.int32, (tc, tc), 0)
    cc = lax.broadcasted_iota(jnp.int32, (tc, tc), 1)
    cum = _mask_dot(r >= cc, lf) + carry_scr[...]
    cum_ref[0] = cum
    cumt_ref[0] = cum.T
    carry_scr[...] = cum[tc - 1:tc, :]


def logf_cumsum(fl, b_f):
    bsz, l, _ = fl.shape
    tc = _row_tile(l, 256)
    return pl.pallas_call(
        functools.partial(_logf_kernel, tc=tc),
        grid=(bsz, l // tc),
        in_specs=[pl.BlockSpec((1, tc, LANES), lambda b, c: (b, c, 0)),
                  pl.BlockSpec((1, LANES), lambda b, c: (0, 0))],
        out_specs=[pl.BlockSpec((1, tc, LANES), lambda b, c: (b, c, 0)),
                   pl.BlockSpec((1, tc, LANES), lambda b, c: (b, c, 0)),
                   pl.BlockSpec((1, LANES, tc), lambda b, c: (b, 0, c))],
        out_shape=[jax.ShapeDtypeStruct((bsz, l, LANES), F32),
                   jax.ShapeDtypeStruct((bsz, l, LANES), F32),
                   jax.ShapeDtypeStruct((bsz, LANES, l), F32)],
        scratch_shapes=[pltpu.VMEM((1, LANES), F32)],
        compiler_params=_params("arbitrary", "arbitrary"),
        name="logf_cumsum",
    )(fl, b_f)


def _flash_kernel(q_ref, k_ref, v_ref, fc_ref, fr_ref, o_ref, m_scr, l_scr, acc_scr, fq_scr, *, t):
    h = pl.program_id(1)
    i = pl.program_id(2)
    j = pl.program_id(3)

    @pl.when(j == 0)
    def _():
        m_scr[...] = jnp.full_like(m_scr, NEG_INF)
        l_scr[...] = jnp.zeros_like(l_scr)
        acc_scr[...] = jnp.zeros_like(acc_scr)
        fc = fc_ref[0]
        lane = lax.broadcasted_iota(jnp.int32, fc.shape, 1)
        fq_scr[...] = jnp.sum(jnp.where(lane == h, fc, 0.0), axis=-1, keepdims=True)

    @pl.when(j <= i)
    def _():
        s = _dot_nt(q_ref[0], k_ref[0].astype(BF16))
        s = s + (fq_scr[...] - fr_ref[0])
        row = lax.broadcasted_iota(jnp.int32, s.shape, 0) + i * t
        col = lax.broadcasted_iota(jnp.int32, s.shape, 1) + j * t
        s = jnp.where(col <= row, s, NEG_INF)
        m_old = m_scr[...]
        m_new = jnp.maximum(m_old, jnp.max(s, axis=-1, keepdims=True))
        alpha = jnp.exp(m_old - m_new)
        p = jnp.exp(s - m_new)
        l_scr[...] = alpha * l_scr[...] + jnp.sum(p, axis=-1, keepdims=True)
        acc_scr[...] = alpha * acc_scr[...] + _dot(p.astype(BF16), v_ref[0].astype(BF16))
        m_scr[...] = m_new

    @pl.when(j == i)
    def _():
        o_ref[0] = (acc_scr[...] / l_scr[...]).astype(o_ref.dtype)


def fox_prompt_attention(q, k, v, cum, cum_t):
    bsz, l, _ = q.shape
    t = _row_tile(l, 1024)
    n = l // t
    kv_spec = pl.BlockSpec((1, t, FOX_HEAD_DIM), lambda b, h, i, j: (b, jnp.minimum(j, i), h))
    return pl.pallas_call(
        functools.partial(_flash_kernel, t=t),
        grid=(bsz, FOX_HEADS, n, n),
        in_specs=[pl.BlockSpec((1, t, FOX_HEAD_DIM), lambda b, h, i, j: (b, i, h)),
                  kv_spec, kv_spec,
                  pl.BlockSpec((1, t, LANES), lambda b, h, i, j: (b, i, 0)),
                  pl.BlockSpec((1, 1, t), lambda b, h, i, j: (b * LANES + h, 0, jnp.minimum(j, i)))],
        out_specs=pl.BlockSpec((1, t, FOX_HEAD_DIM), lambda b, h, i, j: (b, i, h)),
        out_shape=jax.ShapeDtypeStruct((bsz, l, FOX_WIDTH), BF16),
        scratch_shapes=[pltpu.VMEM((t, 1), F32), pltpu.VMEM((t, 1), F32),
                        pltpu.VMEM((t, FOX_HEAD_DIM), F32), pltpu.VMEM((t, 1), F32)],
        compiler_params=_params("arbitrary", "arbitrary", "arbitrary", "arbitrary"),
        name="fox_prompt_attention",
    )(q, k, v, cum, cum_t)


def _decode_kernel(pt_ref, q_ref, kn_ref, vn_ref, lfn_ref, kc_ref, vc_ref, lfc_ref, o_ref,
                   qbd_scr, m_scr, l_scr, acc_scr, carry_scr, *, n_q):
    j = pl.program_id(1)
    head_r = lax.broadcasted_iota(jnp.int32, (FOX_HEADS, FOX_WIDTH), 0) * FOX_HEAD_DIM
    head_c = lax.broadcasted_iota(jnp.int32, (FOX_HEADS, FOX_WIDTH), 1)
    own_head = (head_c >= head_r) & (head_c < head_r + FOX_HEAD_DIM)

    def process(k, v, lf, is_new):
        s = _dot_nt(qbd_scr[...], k.astype(BF16))
        r = lax.broadcasted_iota(jnp.int32, (PAGE_SIZE, PAGE_SIZE), 0)
        c = lax.broadcasted_iota(jnp.int32, (PAGE_SIZE, PAGE_SIZE), 1)
        after = _dot_mask(lf, r > c) + carry_scr[...]
        s = s + jnp.concatenate([after] * n_q, axis=0)
        if is_new:
            row = lax.broadcasted_iota(jnp.int32, s.shape, 0)
            ki = lax.broadcasted_iota(jnp.int32, s.shape, 1)
            s = jnp.where(ki * FOX_HEADS <= row, s, NEG_INF)
        m_old = m_scr[...]
        m_new = jnp.maximum(m_old, jnp.max(s, axis=-1, keepdims=True))
        alpha = jnp.exp(m_old - m_new)
        p = jnp.exp(s - m_new)
        l_scr[...] = alpha * l_scr[...] + jnp.sum(p, axis=-1, keepdims=True)
        acc_scr[...] = alpha * acc_scr[...] + _dot(p.astype(BF16), v.astype(BF16))
        m_scr[...] = m_new
        carry_scr[...] = carry_scr[...] + jnp.sum(lf, axis=-1, keepdims=True)

    @pl.when(j == 0)
    def _():
        m_scr[...] = jnp.full_like(m_scr, NEG_INF)
        l_scr[...] = jnp.zeros_like(l_scr)
        acc_scr[...] = jnp.zeros_like(acc_scr)
        carry_scr[...] = jnp.zeros_like(carry_scr)
        for qi in range(n_q):
            qrow = q_ref[0, qi:qi + 1, :] * (FOX_HEAD_DIM ** -0.5)
            qbd_scr[qi * FOX_HEADS:(qi + 1) * FOX_HEADS, :] = jnp.where(own_head, qrow, 0.0).astype(BF16)
        process(kn_ref[0], vn_ref[0], lfn_ref[0], True)

    @pl.when(j > 0)
    def _():
        process(kc_ref[...], vc_ref[...], lfc_ref[...], False)

    @pl.when(j == pl.num_programs(1) - 1)
    def _():
        o = acc_scr[...] / l_scr[...]
        for qi in range(n_q):
            blk = o[qi * FOX_HEADS:(qi + 1) * FOX_HEADS, :]
            o_ref[0, qi:qi + 1, :] = jnp.sum(jnp.where(own_head, blk, 0.0), axis=0, keepdims=True)


def fox_decode_attention(q, k_new, v_new, lf_new, cache_k, cache_v, cache_lf_t, page_table):
    bsz, n_q, _ = q.shape
    n_pages = page_table.shape[1]

    def page(b, j, pt):
        return pt[b * n_pages + (n_pages - jnp.maximum(j, 1))]

    grid_spec = pltpu.PrefetchScalarGridSpec(
        num_scalar_prefetch=1,
        grid=(bsz, n_pages + 1),
        in_specs=[pl.BlockSpec((1, n_q, FOX_WIDTH), lambda b, j, pt: (b, 0, 0)),
                  pl.BlockSpec((1, PAGE_SIZE, FOX_WIDTH), lambda b, j, pt: (b, 0, 0)),
                  pl.BlockSpec((1, PAGE_SIZE, FOX_WIDTH), lambda b, j, pt: (b, 0, 0)),
                  pl.BlockSpec((1, FOX_HEADS, PAGE_SIZE), lambda b, j, pt: (b, 0, 0)),
                  pl.BlockSpec((None, PAGE_SIZE, FOX_WIDTH), lambda b, j, pt: (page(b, j, pt), 0, 0)),
                  pl.BlockSpec((None, PAGE_SIZE, FOX_WIDTH), lambda b, j, pt: (page(b, j, pt), 0, 0)),
                  pl.BlockSpec((None, FOX_HEADS, PAGE_SIZE), lambda b, j, pt: (page(b, j, pt), 0, 0))],
        out_specs=pl.BlockSpec((1, n_q, FOX_WIDTH), lambda b, j, pt: (b, 0, 0)),
        scratch_shapes=[pltpu.VMEM((n_q * FOX_HEADS, FOX_WIDTH), BF16),
                        pltpu.VMEM((n_q * FOX_HEADS, 1), F32),
                        pltpu.VMEM((n_q * FOX_HEADS, 1), F32),
                        pltpu.VMEM((n_q * FOX_HEADS, FOX_WIDTH), F32),
                        pltpu.VMEM((FOX_HEADS, 1), F32)])
    return pl.pallas_call(
        functools.partial(_decode_kernel, n_q=n_q),
        grid_spec=grid_spec,
        out_shape=jax.ShapeDtypeStruct((bsz, n_q, FOX_WIDTH), F32),
        compiler_params=_params("arbitrary", "arbitrary"),
        name="fox_decode_attention",
    )(page_table.reshape(-1), q, k_new, v_new, lf_new, cache_k, cache_v, cache_lf_t)


def _rms_kernel(x_ref, g_ref, o_ref):
    x = x_ref[...]
    o_ref[...] = x * lax.rsqrt(jnp.mean(x * x, axis=-1, keepdims=True) + EPS) * g_ref[...]


def final_rmsnorm(x, g):
    t, d = x.shape
    tm = _row_tile(t, 1024)
    return pl.pallas_call(
        _rms_kernel,
        grid=(t // tm,),
        in_specs=[pl.BlockSpec((tm, d), lambda i: (i, 0)), pl.BlockSpec((1, d), lambda i: (0, 0))],
        out_specs=pl.BlockSpec((tm, d), lambda i: (i, 0)),
        out_shape=jax.ShapeDtypeStruct((t, d), F32),
        compiler_params=_params("arbitrary"),
        name="final_rmsnorm",
    )(x, g)


def _pad_lanes(v, width=LANES):
    return jnp.pad(v, [(0, 0)] * (v.ndim - 1) + [(0, width - v.shape[-1])])


def _mods(mod_p, mod_s, dec_seq):
    d = D_MODEL
    parts_p = [mod_p[:, None, k * d:(k + 1) * d] for k in range(3)]
    parts_s = [jnp.repeat(mod_s[:, k * d:(k + 1) * d], dec_seq, axis=0)[None] for k in range(3)]
    return parts_p, parts_s


def _moe_layer(y, shift, scale, gate, g, w_router, b_router, w_gu, w_d):
    h, comb = moe_router(y, g, shift, scale, w_router, b_router)
    return moe_experts(h, comb, w_gu, w_d, y, gate)


def kernel(x_prompt, x_sample, cache_k, cache_v, cache_logf, state_ssm, state_ssm_conv, state_ccv,
           page_table, c_prompt, c_sample, ada_w, ada_b, norm_g, ab_w_in, ssm_conv_w, ssm_conv_b,
           ssm_dt_bias, ssm_a_log, ssm_d, ssm_norm_g, ccv_w, ccv_b, ccv_ln_g, ccv_ln_b, ab_w_out,
           fox_w_in, fox_b_f, fox_w_out, moe_w_group, moe_b_group, moe_w_expert, moe_b_expert,
           moe_w_gate, moe_w_up, moe_w_down, final_g):
    d = D_MODEL
    bp, seq, _ = x_prompt.shape
    bs, dec_seq, _ = x_sample.shape
    depth = norm_g.shape[0]
    ts = bs * dec_seq

    n_c = bp + bs
    r_pad = -n_c % SUBLANES
    c_all = jnp.pad(jnp.concatenate([c_prompt, c_sample], axis=0), ((0, r_pad), (0, 0)))
    mod = ada_all(c_all, ada_w.reshape(depth * 2, d, 3 * d), ada_b.reshape(depth * 2, 1, 3 * d))

    yp = x_prompt
    ys = x_sample.reshape(1, ts, d)
    outs = {}
    for layer in range(depth):
        i = layer // 2
        (sh_p, sc_p, gt_p), (sh_s, sc_s, gt_s) = _mods(mod[2 * layer, :bp], mod[2 * layer, bp:n_c], dec_seq)
        g_mix = norm_g[layer, 0][None]
        if layer % 2 == 0:
            w_in = ab_w_in[i]
            o_dt = SSM_D_INNER + SSM_CONV_DIM
            o_u = o_dt + SSM_HEADS
            w_main = jnp.concatenate([w_in[:, :o_dt], w_in[:, o_u:]], axis=1).astype(BF16)
            w_dt = _pad_lanes(w_in[:, o_dt:o_u]).astype(BF16)
            widths = (SSM_D_INNER, SSM_CONV_DIM, 2 * CCV_DIM)
            ssm_w = (ssm_conv_w[i], ssm_conv_b[i][None], _pad_lanes(ssm_dt_bias[i][None]),
                     _pad_lanes(ssm_a_log[i][None]), jnp.repeat(ssm_d[i], SSM_HEAD_DIM)[None],
                     ssm_norm_g[i][None])
            ccv_p = (ccv_w[i], ccv_b[i][None], ccv_ln_g[i][None], ccv_ln_b[i][None])
            w_out = ab_w_out[i].astype(BF16)

            z, xbc, u, dt_raw = nm_matmul(yp, g_mix, sh_p, sc_p, w_main,
                                          [(w, BF16, None) for w in widths], w_aux=w_dt)
            y_ssm, st_p = ssd_mixer(xbc, z, dt_raw, *ssm_w,
                                    jnp.zeros((bp, SUBLANES, SSM_CONV_DIM), F32),
                                    jnp.zeros((bp, SSM_PAIRS, LANES, SSM_STATE), F32), seq, BF16)
            cv, ccv_new_p = ccv_mixer(u, *ccv_p, jnp.zeros((bp, CCV_HALO, CCV_DIM), F32), seq, BF16)
            yp = mm_residual([y_ssm, cv], w_out, yp, gt_p)
            outs["ssm_state_p"] = st_p.reshape(1, bp, SSM_HEADS, SSM_HEAD_DIM, SSM_STATE)
            outs["ssm_conv_p"] = xbc[:, seq - (SSM_CONV - 1):].astype(F32)[None]
            outs["ccv_p"] = ccv_new_p[:, CCV_HALO - (CCV_WIDTH - 1):][None]

            z, xbc, u, dt_raw = nm_matmul(ys, g_mix, sh_s, sc_s, w_main,
                                          [(w, F32, None) for w in widths], w_aux=w_dt)
            per_seq = lambda t: t.reshape(bs, dec_seq, t.shape[-1])
            pad_rows = lambda t, n: jnp.pad(t, ((0, 0), (0, n - t.shape[1]), (0, 0)))
            xbc_s = per_seq(xbc)
            conv_in = jnp.pad(state_ssm_conv[i], ((0, 0), (SUBLANES - (SSM_CONV - 1), 0), (0, 0)))
            y_ssm, st_s = ssd_mixer(pad_rows(xbc_s, SSM_CHUNK), pad_rows(per_seq(z), SSM_CHUNK),
                                    pad_rows(per_seq(dt_raw), SSM_CHUNK), *ssm_w, conv_in,
                                    state_ssm[i].reshape(bs, SSM_PAIRS, LANES, SSM_STATE), dec_seq, F32)
            ccv_in = jnp.pad(state_ccv[i], ((0, 0), (CCV_HALO - (CCV_WIDTH - 1), 0), (0, 0)))
            cv, ccv_new_s = ccv_mixer(pad_rows(per_seq(u), SUBLANES), *ccv_p, ccv_in, dec_seq, F32)
            ys = mm_residual([y_ssm[:, :dec_seq].reshape(1, ts, SSM_D_INNER),
                              cv[:, :dec_seq].reshape(1, ts, CCV_DIM)], w_out, ys, gt_s)
            outs["ssm_state_s"] = st_s.reshape(1, bs, SSM_HEADS, SSM_HEAD_DIM, SSM_STATE)
            outs["ssm_conv_s"] = jnp.concatenate([state_ssm_conv[i], xbc_s], axis=1)[:, -(SSM_CONV - 1):][None]
            outs["ccv_s"] = ccv_new_s[:, CCV_HALO - (CCV_WIDTH - 1):][None]
        else:
            w_in = fox_w_in[i]
            w_main = w_in[:, :3 * FOX_WIDTH].astype(BF16)
            w_f = _pad_lanes(w_in[:, 3 * FOX_WIDTH:]).astype(BF16)
            b_f = _pad_lanes(fox_b_f[i][None])
            w_out = fox_w_out[i].astype(BF16)
            q_scale = FOX_HEAD_DIM ** -0.5

            q, k, v, fl = nm_matmul(yp, g_mix, sh_p, sc_p, w_main,
                                    [(FOX_WIDTH, BF16, q_scale), (FOX_WIDTH, F32, None),
                                     (FOX_WIDTH, F32, None)], w_aux=w_f)
            lf, cum, cum_t = logf_cumsum(fl, b_f)
            att = fox_prompt_attention(q, k, v, cum, cum_t.reshape(bp * LANES, 1, seq))
            yp = mm_residual([att], w_out, yp, gt_p)
            outs["k_p"] = k.reshape(1, bp, seq, FOX_HEADS, FOX_HEAD_DIM)
            outs["v_p"] = v.reshape(1, bp, seq, FOX_HEADS, FOX_HEAD_DIM)
            outs["lf_p"] = lf[:, :, :FOX_HEADS][None]

            q, k, v, fl = nm_matmul(ys, g_mix, sh_s, sc_s, w_main,
                                    [(FOX_WIDTH, F32, None)] * 3, w_aux=w_f)
            lf, _, _ = logf_cumsum(fl, jnp.zeros_like(b_f) + b_f)
            lf_s = lf.reshape(bs, dec_seq, LANES)[:, :, :FOX_HEADS]
            pad_page = lambda t: jnp.pad(t.reshape(bs, dec_seq, FOX_WIDTH),
                                         ((0, 0), (0, PAGE_SIZE - dec_seq), (0, 0)))
            lf_new = jnp.pad(jnp.swapaxes(lf_s, 1, 2), ((0, 0), (0, 0), (0, PAGE_SIZE - dec_seq)))
            n_pool = cache_k.shape[1]
            att = fox_decode_attention(
                q.reshape(bs, dec_seq, FOX_WIDTH), pad_page(k), pad_page(v), lf_new,
                cache_k[i].reshape(n_pool, PAGE_SIZE, FOX_WIDTH),
                cache_v[i].reshape(n_pool, PAGE_SIZE, FOX_WIDTH),
                jnp.swapaxes(cache_logf[i], 1, 2), page_table)
            ys = mm_residual([att.reshape(1, ts, FOX_WIDTH)], w_out, ys, gt_s)
            outs["k_s"] = k.reshape(1, bs, dec_seq, FOX_HEADS, FOX_HEAD_DIM)
            outs["v_s"] = v.reshape(1, bs, dec_seq, FOX_HEADS, FOX_HEAD_DIM)
            outs["lf_s"] = lf_s[None]

        (sh_p, sc_p, gt_p), (sh_s, sc_s, gt_s) = _mods(mod[2 * layer + 1, :bp], mod[2 * layer + 1, bp:n_c],
                                                       dec_seq)
        g_moe = norm_g[layer, 1][None]
        w_router = _pad_lanes(jnp.concatenate([moe_w_expert[layer], moe_w_group[layer]], axis=1))
        b_router = _pad_lanes(jnp.concatenate([moe_b_expert[layer], moe_b_group[layer]])[None])
        w_gu = jnp.concatenate([moe_w_gate[layer], moe_w_up[layer]], axis=-1).astype(BF16)
        w_d = moe_w_down[layer].astype(BF16)
        yp = _moe_layer(yp, sh_p, sc_p, gt_p, g_moe, w_router, b_router, w_gu, w_d)
        ys = _moe_layer(ys, sh_s, sc_s, gt_s, g_moe, w_router, b_router, w_gu, w_d)

    y_prompt = final_rmsnorm(yp.reshape(bp * seq, d), final_g[None]).reshape(bp, seq, d)
    y_sample = final_rmsnorm(ys.reshape(ts, d), final_g[None]).reshape(bs, dec_seq, d)
    return (y_prompt, y_sample, outs["k_p"], outs["v_p"], outs["lf_p"], outs["k_s"], outs["v_s"],
            outs["lf_s"], outs["ssm_state_p"], outs["ssm_state_s"], outs["ssm_conv_p"],
            outs["ssm_conv_s"], outs["ccv_p"], outs["ccv_s"])
```

```python
import functools

import jax
import jax.numpy as jnp
from jax import lax
from jax.experimental import pallas as pl
from jax.experimental.pallas import tpu as pltpu

F32 = jnp.float32
BF16 = jnp.bfloat16
HIGHEST = lax.Precision.HIGHEST

LANES = 128
SUBLANES = 8
VMEM_LIMIT_BYTES = 56 * 1024 * 1024

EPS = 1e-6
D_MODEL = 2048
SSM_HEAD_DIM = 64
SSM_HEADS = 32
SSM_GROUPS = 4
SSM_STATE = 128
SSM_CONV = 4
SSM_CHUNK = 128
SSM_D_INNER = SSM_HEADS * SSM_HEAD_DIM
SSM_BC = SSM_GROUPS * SSM_STATE
SSM_CONV_DIM = SSM_D_INNER + 2 * SSM_BC
SSM_PAIRS = SSM_HEADS // 2
PAIRS_PER_GROUP = SSM_PAIRS // SSM_GROUPS
CCV_DIM = 1024
CCV_WIDTH = 31
CCV_HALO = 32
FOX_HEADS = 16
FOX_HEAD_DIM = 128
FOX_WIDTH = FOX_HEADS * FOX_HEAD_DIM
PAGE_SIZE = 128
DECODE_PAGES_PER_STEP = 4
MOE_GROUPS = 4
MOE_EPG = 4
MOE_EXPERTS = 16
MOE_FF = 512
NEG_INF = float("-inf")


def _params(*sem):
    return pltpu.CompilerParams(dimension_semantics=sem, vmem_limit_bytes=VMEM_LIMIT_BYTES)


def _silu(x):
    return x * jax.nn.sigmoid(x)


def _softplus(x):
    return jnp.maximum(x, 0.0) + jnp.log1p(jnp.exp(-jnp.abs(x)))


def _dot(a, b):
    return jnp.dot(a, b, preferred_element_type=F32)


def _dot_nt(a, b):
    return lax.dot_general(a, b, (((1,), (1,)), ((), ())), preferred_element_type=F32)


def _dot_exact(a, b):
    return jnp.dot(a, b, precision=HIGHEST, preferred_element_type=F32)


def _split3(x):
    hi = x.astype(BF16)
    r1 = x - hi.astype(F32)
    mid = r1.astype(BF16)
    lo = (r1 - mid.astype(F32)).astype(BF16)
    return hi, mid, lo


def _split2(x):
    hi = x.astype(BF16)
    return hi, (x - hi.astype(F32)).astype(BF16)


def _dot3(a_hi, a_lo, b_hi, b_lo, dot=_dot):
    return dot(a_hi, b_hi) + (dot(a_lo, b_hi) + dot(a_hi, b_lo))


def _dot_split2(a, b, dot=_dot):
    return _dot3(*_split2(a), *_split2(b), dot=dot)


def _mask_dot(mask, x):
    mb = mask.astype(F32).astype(BF16)
    hi, mid, lo = _split3(x)
    return _dot(mb, hi) + _dot(mb, mid) + _dot(mb, lo)


def _dot_mask(x, mask):
    mb = mask.astype(F32).astype(BF16)
    hi, mid, lo = _split3(x)
    return _dot(hi, mb) + _dot(mid, mb) + _dot(lo, mb)


def _row_tile(n, pref):
    t = min(n, pref)
    while n % t:
        t //= 2
    return t


def _ada_kernel(c_ref, w_ref, b_ref, o_ref):
    o_ref[0] = _dot(_silu(c_ref[...]).astype(BF16), w_ref[0].astype(BF16)) + b_ref[0]


def ada_all(c_all, w, b):
    r, d = c_all.shape
    s, _, n = w.shape
    tn = 1024
    return pl.pallas_call(
        _ada_kernel,
        grid=(s, n // tn),
        in_specs=[pl.BlockSpec((r, d), lambda i, j: (0, 0)),
                  pl.BlockSpec((1, d, tn), lambda i, j: (i, 0, j)),
                  pl.BlockSpec((1, 1, tn), lambda i, j: (i, 0, j))],
        out_specs=pl.BlockSpec((1, r, tn), lambda i, j: (i, 0, j)),
        out_shape=jax.ShapeDtypeStruct((s, r, n), F32),
        compiler_params=_params("arbitrary", "arbitrary"),
        name="ada_all",
    )(c_all, w, b)


def _mod_spec(mod, tm, tn=None):
    if tn is None:
        d = mod.shape[-1]
        if mod.shape[1] == 1:
            return pl.BlockSpec((1, 1, d), lambda b, i, j: (b, 0, 0))
        return pl.BlockSpec((1, tm, d), lambda b, i, j: (b, i, 0))
    if mod.shape[1] == 1:
        return pl.BlockSpec((1, 1, tn), lambda b, i, j: (b, 0, j))
    return pl.BlockSpec((1, tm, tn), lambda b, i, j: (b, i, j))


def _norm_mod(x, g, shift, scale):
    ms = jnp.mean(x * x, axis=-1, keepdims=True)
    return (x * lax.rsqrt(ms + EPS) * g) * (1.0 + scale) + shift


def _nm_matmul_kernel(x_ref, g_ref, sh_ref, sc_ref, w_ref, *rest, ranges, scales, has_aux):
    n_out = len(ranges)
    if has_aux:
        waux_ref = rest[0]
        outs = rest[1:1 + n_out]
        aux_ref = rest[1 + n_out]
    else:
        outs = rest[:n_out]
    h_scr = rest[-1]
    j = pl.program_id(2)

    precise = w_ref.shape[0] == 2

    def project(wr):
        if precise:
            return _dot3(h_scr[0], h_scr[1], wr[0], wr[1])
        return _dot(h_scr[0], wr[0])

    @pl.when(j == 0)
    def _():
        h = _norm_mod(x_ref[0], g_ref[...], sh_ref[0], sc_ref[0])
        h_scr[0] = h.astype(BF16)
        if precise:
            h_scr[1] = (h - h_scr[0].astype(F32)).astype(BF16)
        if has_aux:
            aux_ref[0] = project(waux_ref)

    res = project(w_ref)
    for (lo, hi), scale, o_ref in zip(ranges, scales, outs):
        @pl.when((j >= lo) & (j < hi))
        def _(o_ref=o_ref, scale=scale):
            val = res if scale is None else res * scale
            o_ref[0] = val.astype(o_ref.dtype)


def nm_matmul(x, g, shift, scale, w, outs, w_aux=None, precise=False, tm_pref=512, tn=512):
    bsz, l, d = x.shape
    n = w.shape[-1]
    parts = 2 if precise else 1
    tm = _row_tile(l, tm_pref)
    ranges, scales, out_shapes, out_specs = [], [], [], []
    lo = 0
    for width, dtype, sc in outs:
        nt = width // tn
        ranges.append((lo, lo + nt))
        scales.append(sc)
        out_shapes.append(jax.ShapeDtypeStruct((bsz, l, width), dtype))
        out_specs.append(pl.BlockSpec(
            (1, tm, tn), lambda b, i, j, lo=lo, nt=nt: (b, i, jnp.clip(j - lo, 0, nt - 1))))
        lo += nt
    assert lo * tn == n
    in_specs = [pl.BlockSpec((1, tm, d), lambda b, i, j: (b, i, 0)),
                pl.BlockSpec((1, d), lambda b, i, j: (0, 0)),
                _mod_spec(shift, tm), _mod_spec(scale, tm),
                pl.BlockSpec((parts, d, tn), lambda b, i, j: (0, 0, j))]
    args = [x, g, shift, scale, w]
    if w_aux is not None:
        in_specs.append(pl.BlockSpec((parts, d, LANES), lambda b, i, j: (0, 0, 0)))
        args.append(w_aux)
        out_shapes.append(jax.ShapeDtypeStruct((bsz, l, LANES), F32))
        out_specs.append(pl.BlockSpec((1, tm, LANES), lambda b, i, j: (b, i, 0)))
    return pl.pallas_call(
        functools.partial(_nm_matmul_kernel, ranges=tuple(ranges), scales=tuple(scales),
                          has_aux=w_aux is not None),
        grid=(bsz, l // tm, n // tn),
        in_specs=in_specs, out_specs=out_specs, out_shape=out_shapes,
        scratch_shapes=[pltpu.VMEM((parts, tm, d), BF16)],
        compiler_params=_params("arbitrary", "arbitrary", "arbitrary"),
        name="nm_matmul",
    )(*args)


def _ssd_kernel(xbc_ref, z_ref, dt_ref, cw_ref, cb_ref, dtb_ref, alog_ref, dsk_ref, ng_ref,
                ic_ref, is_ref, y_ref, fs_ref, ext_scr, xc_scr, st_scr, y_scr, *, q, l_valid, precise):
    c = pl.program_id(1)
    nc = pl.num_programs(1)

    def mm(a, b, dot):
        if precise:
            return _dot_split2(a, b, dot)
        return dot(a.astype(BF16), b.astype(BF16))

    @pl.when(c == 0)
    def _():
        ext_scr[0:SUBLANES, :] = ic_ref[0]
        st_scr[...] = is_ref[0]

    @pl.when(c > 0)
    def _():
        ext_scr[0:SUBLANES, :] = ext_scr[q:q + SUBLANES, :]

    ext_scr[SUBLANES:SUBLANES + q, :] = xbc_ref[0].astype(F32)
    conv = cb_ref[...]
    for k in range(SSM_CONV):
        off = SUBLANES - (SSM_CONV - 1) + k
        conv = conv + cw_ref[k:k + 1, :] * ext_scr[off:off + q, :]
    xc_scr[...] = _silu(conv)

    row = lax.broadcasted_iota(jnp.int32, (q, LANES), 0)
    lane = lax.broadcasted_iota(jnp.int32, (q, LANES), 1)
    left = lane < SSM_HEAD_DIM
    dt = _softplus(dt_ref[0] + dtb_ref[...])
    dt = jnp.where(row + c * q < l_valid, dt, 0.0)
    a = dt * (-jnp.exp(alog_ref[...]))
    rq = lax.broadcasted_iota(jnp.int32, (q, q), 0)
    cq = lax.broadcasted_iota(jnp.int32, (q, q), 1)
    causal = rq >= cq
    a_cum = _mask_dot(causal, a)
    a_cum_t = a_cum.T
    a_last = a_cum[q - 1:q, :]

    for g in range(SSM_GROUPS):
        b_lo = SSM_D_INNER + g * SSM_STATE
        c_lo = SSM_D_INNER + SSM_BC + g * SSM_STATE
        bm = xc_scr[:, b_lo:b_lo + SSM_STATE]
        cm = xc_scr[:, c_lo:c_lo + SSM_STATE]
        if not precise:
            bm, cm = bm.astype(BF16), cm.astype(BF16)
        cb = mm(cm, bm, _dot_nt)
        for pp in range(PAIRS_PER_GROUP):
            p = g * PAIRS_PER_GROUP + pp
            h0, h1 = 2 * p, 2 * p + 1
            col0, col1 = a_cum[:, h0:h0 + 1], a_cum[:, h1:h1 + 1]
            l0 = jnp.exp(jnp.where(causal, col0 - a_cum_t[h0:h0 + 1, :], NEG_INF))
            l1 = jnp.exp(jnp.where(causal, col1 - a_cum_t[h1:h1 + 1, :], NEG_INF))
            m = jnp.concatenate([cb * l0, cb * l1], axis=1)
            xs = xc_scr[:, p * LANES:(p + 1) * LANES]
            xd = xs * jnp.where(left, dt[:, h0:h0 + 1], dt[:, h1:h1 + 1])
            xd_bd = jnp.concatenate([jnp.where(left, xd, 0.0), jnp.where(left, 0.0, xd)],
                                    axis=0)
            y = mm(m, xd_bd, _dot)
            st = st_scr[p]
            y = y + mm(cm, st, _dot_nt) * jnp.exp(jnp.where(left, col0, col1))
            y_scr[:, p * LANES:(p + 1) * LANES] = y + dsk_ref[:, p * LANES:(p + 1) * LANES] * xs
            al0, al1 = a_last[:, h0:h0 + 1], a_last[:, h1:h1 + 1]
            dte = jnp.exp(jnp.where(left, al0 - col0, al1 - col1))
            upd = mm((xd * dte).T, bm, _dot)
            half = (SSM_HEAD_DIM, SSM_STATE)
            decay = jnp.exp(jnp.concatenate([jnp.broadcast_to(al0, half), jnp.broadcast_to(al1, half)],
                                            axis=0))
            st_scr[p] = st * decay + upd

    gw = SSM_D_INNER // SSM_GROUPS
    for g in range(SSM_GROUPS):
        sl = slice(g * gw, (g + 1) * gw)
        v = y_scr[:, sl] * _silu(z_ref[0, :, sl].astype(F32))
        v = v * lax.rsqrt(jnp.mean(v * v, axis=-1, keepdims=True) + EPS)
        y_ref[0, :, sl] = (v * ng_ref[:, sl]).astype(y_ref.dtype)

    @pl.when(c == nc - 1)
    def _():
        fs_ref[0] = st_scr[...]


def ssd_mixer(xbc, z, dt_raw, conv_w, conv_b, dt_bias, a_log, d_skip, norm_g, init_conv, init_state,
              l_valid, out_dtype, precise=False):
    bsz, l, _ = xbc.shape
    q = SSM_CHUNK
    assert l % q == 0
    vec = lambda n: pl.BlockSpec((1, n), lambda b, c: (0, 0))
    state_spec = pl.BlockSpec((1, SSM_PAIRS, LANES, SSM_STATE), lambda b, c: (b, 0, 0, 0))
    return pl.pallas_call(
        functools.partial(_ssd_kernel, q=q, l_valid=l_valid, precise=precise),
        grid=(bsz, l // q),
        in_specs=[pl.BlockSpec((1, q, SSM_CONV_DIM), lambda b, c: (b, c, 0)),
                  pl.BlockSpec((1, q, SSM_D_INNER), lambda b, c: (b, c, 0)),
                  pl.BlockSpec((1, q, LANES), lambda b, c: (b, c, 0)),
                  pl.BlockSpec((SSM_CONV, SSM_CONV_DIM), lambda b, c: (0, 0)),
                  vec(SSM_CONV_DIM), vec(LANES), vec(LANES), vec(SSM_D_INNER), vec(SSM_D_INNER),
                  pl.BlockSpec((1, SUBLANES, SSM_CONV_DIM), lambda b, c: (b, 0, 0)),
                  state_spec],
        out_specs=[pl.BlockSpec((1, q, SSM_D_INNER), lambda b, c: (b, c, 0)), state_spec],
        out_shape=[jax.ShapeDtypeStruct((bsz, l, SSM_D_INNER), out_dtype),
                   jax.ShapeDtypeStruct((bsz, SSM_PAIRS, LANES, SSM_STATE), F32)],
        scratch_shapes=[pltpu.VMEM((q + SUBLANES, SSM_CONV_DIM), F32),
                        pltpu.VMEM((q, SSM_CONV_DIM), F32),
                        pltpu.VMEM((SSM_PAIRS, LANES, SSM_STATE), F32),
                        pltpu.VMEM((q, SSM_D_INNER), F32)],
        compiler_params=_params("arbitrary", "arbitrary"),
        name="ssd_mixer",
    )(xbc, z, dt_raw, conv_w, conv_b, dt_bias, a_log, d_skip, norm_g, init_conv, init_state)


def _ccv_kernel(u_ref, w_ref, b_ref, lg_ref, lb_ref, init_ref, o_ref, new_ref, ext_scr, cv_scr,
                win_scr, *, tl, lv_last, rb):
    t = pl.program_id(1)
    nt = pl.num_programs(1)

    @pl.when(t == 0)
    def _():
        ext_scr[0:CCV_HALO, :] = init_ref[0]

    @pl.when(t > 0)
    def _():
        ext_scr[0:CCV_HALO, :] = ext_scr[tl:tl + CCV_HALO, :]

    ua = u_ref[0, :, 0:CCV_DIM].astype(F32)
    ug = u_ref[0, :, CCV_DIM:2 * CCV_DIM].astype(F32)
    ext_scr[CCV_HALO:CCV_HALO + tl, :] = ua * jax.nn.sigmoid(ug)

    first = CCV_HALO - (CCV_WIDTH - 1)

    def body(r, carry):
        r0 = pl.multiple_of(r * rb, rb)
        win_scr[...] = ext_scr[pl.ds(r0, rb + CCV_HALO), :]
        acc = jnp.broadcast_to(b_ref[...], (rb, CCV_DIM))
        for k in range(CCV_WIDTH):
            acc = acc + w_ref[k:k + 1, :] * win_scr[first + k:first + k + rb, :]
        cv_scr[pl.ds(r0, rb), :] = acc
        return carry

    lax.fori_loop(0, tl // rb, body, 0)
    cv = cv_scr[...]
    mu = jnp.mean(cv, axis=-1, keepdims=True)
    var = jnp.mean(jnp.square(cv - mu), axis=-1, keepdims=True)
    y = (cv - mu) * lax.rsqrt(var + EPS) * lg_ref[...] + lb_ref[...]
    o_ref[0] = _silu(y).astype(o_ref.dtype)

    @pl.when(t == nt - 1)
    def _():
        new_ref[0] = ext_scr[lv_last:lv_last + CCV_HALO, :]


def ccv_mixer(u, w, b, ln_g, ln_b, init, l_valid, out_dtype):
    bsz, l, _ = u.shape
    tl = _row_tile(l, 512)
    lv_last = l_valid - (l - tl)
    rb = min(tl, 32)
    assert 0 < lv_last <= tl and (tl >= CCV_HALO or l == tl)
    vec = lambda: pl.BlockSpec((1, CCV_DIM), lambda bb, t: (0, 0))
    return pl.pallas_call(
        functools.partial(_ccv_kernel, tl=tl, lv_last=lv_last, rb=rb),
        grid=(bsz, l // tl),
        in_specs=[pl.BlockSpec((1, tl, 2 * CCV_DIM), lambda bb, t: (bb, t, 0)),
                  pl.BlockSpec((CCV_WIDTH, CCV_DIM), lambda bb, t: (0, 0)),
                  vec(), vec(), vec(),
                  pl.BlockSpec((1, CCV_HALO, CCV_DIM), lambda bb, t: (bb, 0, 0))],
        out_specs=[pl.BlockSpec((1, tl, CCV_DIM), lambda bb, t: (bb, t, 0)),
                   pl.BlockSpec((1, CCV_HALO, CCV_DIM), lambda bb, t: (bb, 0, 0))],
        out_shape=[jax.ShapeDtypeStruct((bsz, l, CCV_DIM), out_dtype),
                   jax.ShapeDtypeStruct((bsz, CCV_HALO, CCV_DIM), F32)],
        scratch_shapes=[pltpu.VMEM((tl + 2 * CCV_HALO, CCV_DIM), F32),
                        pltpu.VMEM((tl, CCV_DIM), F32),
                        pltpu.VMEM((rb + CCV_HALO, CCV_DIM), F32)],
        compiler_params=_params("arbitrary", "arbitrary"),
        name="ccv_mixer",
    )(u, w, b, ln_g, ln_b, init)


def _mm_res_kernel(*refs, n_lhs):
    a_refs = refs[:n_lhs]
    w_refs = refs[n_lhs:2 * n_lhs]
    res_ref, gate_ref, o_ref = refs[2 * n_lhs:]

    def project(a_ref, w_ref):
        if w_ref.shape[0] == 2:
            return _dot3(*_split2(a_ref[0].astype(F32)), w_ref[0], w_ref[1])
        return _dot(a_ref[0].astype(BF16), w_ref[0])

    acc = project(a_refs[0], w_refs[0])
    for a_ref, w_ref in zip(a_refs[1:], w_refs[1:]):
        acc = acc + project(a_ref, w_ref)
    o_ref[0] = res_ref[0] + gate_ref[0] * acc


def mm_residual(lhs, w, res, gate, precise=False, tm_pref=1024, tn=512):
    bsz, l, n = res.shape
    parts = 2 if precise else 1
    tm = _row_tile(l, tm_pref)
    in_specs, w_specs = [], []
    row = 0
    for a in lhs:
        k = a.shape[-1]
        assert row % k == 0
        in_specs.append(pl.BlockSpec((1, tm, k), lambda b, i, j: (b, i, 0)))
        w_specs.append(pl.BlockSpec((parts, k, tn), lambda b, i, j, rb=row // k: (0, rb, j)))
        row += k
    assert row == w.shape[1]
    in_specs += w_specs + [pl.BlockSpec((1, tm, tn), lambda b, i, j: (b, i, j)), _mod_spec(gate, tm, tn)]
    return pl.pallas_call(
        functools.partial(_mm_res_kernel, n_lhs=len(lhs)),
        grid=(bsz, l // tm, n // tn),
        in_specs=in_specs,
        out_specs=pl.BlockSpec((1, tm, tn), lambda b, i, j: (b, i, j)),
        out_shape=jax.ShapeDtypeStruct((bsz, l, n), F32),
        compiler_params=_params("arbitrary", "arbitrary", "arbitrary"),
        name="mm_residual",
    )(*lhs, *([w] * len(lhs)), res, gate)


def _router_kernel(x_ref, g_ref, sh_ref, sc_ref, wr_ref, br_ref, h_ref, comb_ref):
    h = _norm_mod(x_ref[0], g_ref[...], sh_ref[0], sc_ref[0])
    h_ref[0] = h.astype(h_ref.dtype)
    logits = _dot(h.astype(BF16), wr_ref[...].astype(BF16)) + br_ref[...]
    tm = logits.shape[0]
    lane = lax.broadcasted_iota(jnp.int32, (tm, LANES), 1).astype(F32)
    first_of = lambda hit: jnp.min(jnp.where(hit, lane, float(LANES)), axis=-1, keepdims=True)
    is_group = (lane >= MOE_EXPERTS) & (lane < MOE_EXPERTS + MOE_GROUPS)
    gl = jnp.where(is_group, logits, NEG_INF)
    gmax = jnp.max(gl, axis=-1, keepdims=True)
    gsum = jnp.sum(jnp.exp(gl - gmax), axis=-1, keepdims=True)
    g_val = 1.0 / gsum
    g_idx = first_of(gl == gmax) - MOE_EXPERTS
    in_group = (lane >= g_idx * MOE_EPG) & (lane < (g_idx + 1.0) * MOE_EPG)
    e1 = jnp.where(in_group, logits, NEG_INF)
    m1 = jnp.max(e1, axis=-1, keepdims=True)
    i1 = first_of(e1 == m1)
    e2 = jnp.where(lane == i1, NEG_INF, e1)
    m2 = jnp.max(e2, axis=-1, keepdims=True)
    i2 = first_of(e2 == m2)
    r = jnp.exp(m2 - m1)
    w1 = g_val / (1.0 + r)
    w2 = g_val * r / (1.0 + r)
    comb_ref[0] = jnp.where(lane == i1, w1, 0.0) + jnp.where(lane == i2, w2, 0.0)


def moe_router(x, g, shift, scale, w_router, b_router, tm_pref=512):
    bsz, l, d = x.shape
    tm = _row_tile(l, tm_pref)
    return pl.pallas_call(
        _router_kernel,
        grid=(bsz, l // tm, 1),
        in_specs=[pl.BlockSpec((1, tm, d), lambda b, i, j: (b, i, 0)),
                  pl.BlockSpec((1, d), lambda b, i, j: (0, 0)),
                  _mod_spec(shift, tm), _mod_spec(scale, tm),
                  pl.BlockSpec((d, LANES), lambda b, i, j: (0, 0)),
                  pl.BlockSpec((1, LANES), lambda b, i, j: (0, 0))],
        out_specs=[pl.BlockSpec((1, tm, d), lambda b, i, j: (b, i, 0)),
                   pl.BlockSpec((1, tm, LANES), lambda b, i, j: (b, i, 0))],
        out_shape=[jax.ShapeDtypeStruct((bsz, l, d), BF16),
                   jax.ShapeDtypeStruct((bsz, l, LANES), F32)],
        compiler_params=_params("arbitrary", "arbitrary", "arbitrary"),
        name="moe_router",
    )(x, g, shift, scale, w_router, b_router)


def _moe_kernel(h_ref, comb_ref, wgu_ref, wd_ref, res_ref, gate_ref, o_ref, acc_scr):
    e = pl.program_id(2)

    @pl.when(e == 0)
    def _():
        acc_scr[...] = jnp.zeros_like(acc_scr)

    comb = comb_ref[0]
    lane = lax.broadcasted_iota(jnp.int32, comb.shape, 1)
    col = jnp.sum(jnp.where(lane == e, comb, 0.0), axis=-1, keepdims=True)

    @pl.when(jnp.max(jnp.abs(col)) > 0.0)
    def _():
        gu = _dot(h_ref[0], wgu_ref[0])
        act = _silu(gu[:, :MOE_FF]) * gu[:, MOE_FF:] * col
        acc_scr[...] += _dot(act.astype(BF16), wd_ref[0])

    @pl.when(e == pl.num_programs(2) - 1)
    def _():
        o_ref[0] = res_ref[0] + gate_ref[0] * acc_scr[...]


def moe_experts(h, comb, w_gu, w_d, res, gate, tm_pref=512):
    bsz, l, d = h.shape
    tm = _row_tile(l, tm_pref)
    e = w_gu.shape[0]
    if gate.shape[1] == 1:
        gate_spec = pl.BlockSpec((1, 1, d), lambda b, i, j: (b, 0, 0))
    else:
        gate_spec = pl.BlockSpec((1, tm, d), lambda b, i, j: (b, i, 0))
    return pl.pallas_call(
        _moe_kernel,
        grid=(bsz, l // tm, e),
        in_specs=[pl.BlockSpec((1, tm, d), lambda b, i, j: (b, i, 0)),
                  pl.BlockSpec((1, tm, LANES), lambda b, i, j: (b, i, 0)),
                  pl.BlockSpec((1, d, 2 * MOE_FF), lambda b, i, j: (j, 0, 0)),
                  pl.BlockSpec((1, MOE_FF, d), lambda b, i, j: (j, 0, 0)),
                  pl.BlockSpec((1, tm, d), lambda b, i, j: (b, i, 0)),
                  gate_spec],
        out_specs=pl.BlockSpec((1, tm, d), lambda b, i, j: (b, i, 0)),
        out_shape=jax.ShapeDtypeStruct((bsz, l, d), F32),
        scratch_shapes=[pltpu.VMEM((tm, d), F32)],
        compiler_params=_params("arbitrary", "arbitrary", "arbitrary"),
        name="moe_experts",
    )(h, comb, w_gu, w_d, res, gate)


def _logf_kernel(fl_ref, bf_ref, lf_ref, cum_ref, cumt_ref, carry_scr, *, tc):
    c = pl.program_id(1)

    @pl.when(c == 0)
    def _():
        carry_scr[...] = jnp.zeros_like(carry_scr)

    x = fl_ref[0] + bf_ref[...]
    lf = jnp.minimum(x, 0.0) - jnp.log1p(jnp.exp(-jnp.abs(x)))
    lf_ref[0] = lf
    r = lax.broadcasted_iota(jnp.int32, (tc, tc), 0)
    cc = lax.broadcasted_iota(jnp.int32, (tc, tc), 1)
    cum = _mask_dot(r >= cc, lf) + carry_scr[...]
    cum_ref[0] = cum
    cumt_ref[0] = cum.T
    carry_scr[...] = cum[tc - 1:tc, :]


def logf_cumsum(fl, b_f):
    bsz, l, _ = fl.shape
    tc = _row_tile(l, 256)
    return pl.pallas_call(
        functools.partial(_logf_kernel, tc=tc),
        grid=(bsz, l // tc),
        in_specs=[pl.BlockSpec((1, tc, LANES), lambda b, c: (b, c, 0)),
                  pl.BlockSpec((1, LANES), lambda b, c: (0, 0))],
        out_specs=[pl.BlockSpec((1, tc, LANES), lambda b, c: (b, c, 0)),
                   pl.BlockSpec((1, tc, LANES), lambda b, c: (b, c, 0)),
                   pl.BlockSpec((1, LANES, tc), lambda b, c: (b, 0, c))],
        out_shape=[jax.ShapeDtypeStruct((bsz, l, LANES), F32),
                   jax.ShapeDtypeStruct((bsz, l, LANES), F32),
                   jax.ShapeDtypeStruct((bsz, LANES, l), F32)],
        scratch_shapes=[pltpu.VMEM((1, LANES), F32)],
        compiler_params=_params("arbitrary", "arbitrary"),
        name="logf_cumsum",
    )(fl, b_f)


def _flash_kernel(q_ref, k_ref, v_ref, fc_ref, fr_ref, o_ref, m_scr, l_scr, acc_scr, fq_scr):
    h = pl.program_id(1)
    i = pl.program_id(2)
    j = pl.program_id(3)

    @pl.when(j == 0)
    def _():
        m_scr[...] = jnp.full_like(m_scr, NEG_INF)
        l_scr[...] = jnp.zeros_like(l_scr)
        acc_scr[...] = jnp.zeros_like(acc_scr)
        fc = fc_ref[0]
        lane = lax.broadcasted_iota(jnp.int32, fc.shape, 1)
        fq_scr[...] = jnp.sum(jnp.where(lane == h, fc, 0.0), axis=-1, keepdims=True)

    def step(on_diagonal):
        s = _dot_nt(q_ref[0], k_ref[0].astype(BF16))
        s = s + (fq_scr[...] - fr_ref[0])
        if on_diagonal:
            row = lax.broadcasted_iota(jnp.int32, s.shape, 0)
            col = lax.broadcasted_iota(jnp.int32, s.shape, 1)
            s = jnp.where(col <= row, s, NEG_INF)
        m_old = m_scr[...]
        m_new = jnp.maximum(m_old, jnp.max(s, axis=-1, keepdims=True))
        alpha = jnp.exp(m_old - m_new)
        p = jnp.exp(s - m_new)
        l_scr[...] = alpha * l_scr[...] + jnp.sum(p, axis=-1, keepdims=True)
        acc_scr[...] = alpha * acc_scr[...] + _dot(p.astype(BF16), v_ref[0].astype(BF16))
        m_scr[...] = m_new

    @pl.when(j < i)
    def _():
        step(False)

    @pl.when(j == i)
    def _():
        step(True)
        o_ref[0] = (acc_scr[...] / l_scr[...]).astype(o_ref.dtype)


def fox_prompt_attention(q, k, v, cum, cum_t):
    bsz, l, _ = q.shape
    t = _row_tile(l, 1024)
    n = l // t
    kv_spec = pl.BlockSpec((1, t, FOX_HEAD_DIM), lambda b, h, i, j: (b, jnp.minimum(j, i), h))
    return pl.pallas_call(
        _flash_kernel,
        grid=(bsz, FOX_HEADS, n, n),
        in_specs=[pl.BlockSpec((1, t, FOX_HEAD_DIM), lambda b, h, i, j: (b, i, h)),
                  kv_spec, kv_spec,
                  pl.BlockSpec((1, t, LANES), lambda b, h, i, j: (b, i, 0)),
                  pl.BlockSpec((1, 1, t), lambda b, h, i, j: (b * LANES + h, 0, jnp.minimum(j, i)))],
        out_specs=pl.BlockSpec((1, t, FOX_HEAD_DIM), lambda b, h, i, j: (b, i, h)),
        out_shape=jax.ShapeDtypeStruct((bsz, l, FOX_WIDTH), BF16),
        scratch_shapes=[pltpu.VMEM((t, 1), F32), pltpu.VMEM((t, 1), F32),
                        pltpu.VMEM((t, FOX_HEAD_DIM), F32), pltpu.VMEM((t, 1), F32)],
        compiler_params=_params("arbitrary", "arbitrary", "arbitrary", "arbitrary"),
        name="fox_prompt_attention",
    )(q, k, v, cum, cum_t)


def _decode_kernel(pt_ref, q_ref, kn_ref, vn_ref, lfn_ref, *rest, n_pp):
    kc_refs, vc_refs, lfc_refs = rest[:n_pp], rest[n_pp:2 * n_pp], rest[2 * n_pp:3 * n_pp]
    o_ref, qb_scr, m_scr, l_scr, acc_scr, carry_scr = rest[3 * n_pp:]
    j = pl.program_id(1)
    n_rows = q_ref.shape[1]
    li = lax.broadcasted_iota(jnp.int32, (LANES, LANES), 0)
    lj = lax.broadcasted_iota(jnp.int32, (LANES, LANES), 1)
    head_mask = FOX_HEADS - 1
    same_head = (li & head_mask) == (lj & head_mask)
    row = lax.broadcasted_iota(jnp.int32, (n_rows, LANES), 0)
    lane = lax.broadcasted_iota(jnp.int32, (n_rows, LANES), 1)
    own = (row & head_mask) == (lane & head_mask)

    def process(ks, vs, lf, is_new):
        per_page = ks[0].shape[0] // LANES
        n_blk = per_page * len(ks)
        qb = qb_scr[...]
        scores = [_dot_nt(qb, k.astype(BF16)) for k in ks]
        within = _dot_mask(lf, same_head & (li > lj))
        tot = _dot_mask(lf, same_head)
        after = within + carry_scr[...]
        if n_blk > 1:
            ri = lax.broadcasted_iota(jnp.int32, (n_blk, n_blk), 0)
            rj = lax.broadcasted_iota(jnp.int32, (n_blk, n_blk), 1)
            after = after + _mask_dot(rj > ri, tot)
        valid = own & (lane <= row) if is_new else own
        blocks = []
        for r in range(n_blk):
            s_blk = scores[r // per_page][:, (r % per_page) * LANES:(r % per_page + 1) * LANES]
            blocks.append(jnp.where(valid, s_blk + after[r:r + 1, :], NEG_INF))
        blk_max = blocks[0]
        for blk in blocks[1:]:
            blk_max = jnp.maximum(blk_max, blk)
        m_old = m_scr[...]
        m_new = jnp.maximum(m_old, jnp.max(blk_max, axis=-1, keepdims=True))
        alpha = jnp.exp(m_old - m_new)
        probs = [jnp.exp(blk - m_new) for blk in blocks]
        blk_sum = probs[0]
        for pr in probs[1:]:
            blk_sum = blk_sum + pr
        l_scr[...] = alpha * l_scr[...] + jnp.sum(blk_sum, axis=-1, keepdims=True)
        acc = alpha * acc_scr[...]
        for i, v in enumerate(vs):
            p = jnp.concatenate([pr.astype(BF16) for pr in probs[i * per_page:(i + 1) * per_page]], axis=1)
            acc = acc + _dot(p, v.astype(BF16))
        acc_scr[...] = acc
        m_scr[...] = m_new
        carry_scr[...] = carry_scr[...] + jnp.sum(tot, axis=0, keepdims=True)

    @pl.when(j == 0)
    def _():
        m_scr[...] = jnp.full_like(m_scr, NEG_INF)
        l_scr[...] = jnp.zeros_like(l_scr)
        acc_scr[...] = jnp.zeros_like(acc_scr)
        carry_scr[...] = jnp.zeros_like(carry_scr)
        qb_scr[...] = (q_ref[0] * (FOX_HEAD_DIM ** -0.5)).astype(BF16)
        process([kn_ref[0]], [vn_ref[0]], lfn_ref[0], True)

    @pl.when(j > 0)
    def _():
        process([r[...] for r in kc_refs], [r[...] for r in vc_refs],
                jnp.concatenate([r[...] for r in lfc_refs], axis=0), False)

    @pl.when(j == pl.num_programs(1) - 1)
    def _():
        o_ref[0] = acc_scr[...] / l_scr[...]


def fox_decode_attention(q, k_new, v_new, lf_new, cache_k, cache_v, cache_lf, page_table):
    bsz, n_rows, _ = q.shape
    n_pages = page_table.shape[1]
    page_rows = PAGE_SIZE * FOX_HEADS
    n_pp = DECODE_PAGES_PER_STEP
    assert n_pages % n_pp == 0

    def per_page(rows, slot):
        def index(b, j, pt):
            return pt[b * n_pages + n_pages - jnp.maximum(j, 1) * n_pp + slot], 0, 0
        return pl.BlockSpec((None, rows, LANES), index)

    per_seq = lambda rows: pl.BlockSpec((1, rows, LANES), lambda b, j, pt: (b, 0, 0))
    slots = range(n_pp)
    grid_spec = pltpu.PrefetchScalarGridSpec(
        num_scalar_prefetch=1,
        grid=(bsz, n_pages // n_pp + 1),
        in_specs=[per_seq(n_rows), per_seq(LANES), per_seq(LANES), per_seq(SUBLANES)]
        + [per_page(page_rows, s) for s in slots] + [per_page(page_rows, s) for s in slots]
        + [per_page(page_rows // LANES, s) for s in slots],
        out_specs=per_seq(n_rows),
        scratch_shapes=[pltpu.VMEM((n_rows, LANES), BF16),
                        pltpu.VMEM((n_rows, 1), F32),
                        pltpu.VMEM((n_rows, 1), F32),
                        pltpu.VMEM((n_rows, LANES), F32),
                        pltpu.VMEM((1, LANES), F32)])
    return pl.pallas_call(
        functools.partial(_decode_kernel, n_pp=n_pp),
        grid_spec=grid_spec,
        out_shape=jax.ShapeDtypeStruct((bsz, n_rows, LANES), F32),
        compiler_params=_params("arbitrary", "arbitrary"),
        name="fox_decode_attention",
    )(page_table.reshape(-1), q, k_new, v_new, lf_new, *([cache_k] * n_pp), *([cache_v] * n_pp),
      *([cache_lf] * n_pp))


def _rms_kernel(x_ref, g_ref, o_ref):
    x = x_ref[...]
    o_ref[...] = x * lax.rsqrt(jnp.mean(x * x, axis=-1, keepdims=True) + EPS) * g_ref[...]


def final_rmsnorm(x, g):
    t, d = x.shape
    tm = _row_tile(t, 1024)
    return pl.pallas_call(
        _rms_kernel,
        grid=(t // tm,),
        in_specs=[pl.BlockSpec((tm, d), lambda i: (i, 0)), pl.BlockSpec((1, d), lambda i: (0, 0))],
        out_specs=pl.BlockSpec((tm, d), lambda i: (i, 0)),
        out_shape=jax.ShapeDtypeStruct((t, d), F32),
        compiler_params=_params("arbitrary"),
        name="final_rmsnorm",
    )(x, g)


def _pad_lanes(v, width=LANES):
    return jnp.pad(v, [(0, 0)] * (v.ndim - 1) + [(0, width - v.shape[-1])])


def _hi_lo(w, with_lo=False):
    hi = w.astype(BF16)
    if not with_lo:
        return hi[None]
    return jnp.stack([hi, (w - hi.astype(F32)).astype(BF16)])


def _mods(mod_p, mod_s, dec_seq):
    d = D_MODEL
    parts_p = [mod_p[:, None, k * d:(k + 1) * d] for k in range(3)]
    parts_s = [jnp.repeat(mod_s[:, k * d:(k + 1) * d], dec_seq, axis=0)[None] for k in range(3)]
    return parts_p, parts_s


def _moe_layer(y, shift, scale, gate, g, w_router, b_router, w_gu, w_d):
    h, comb = moe_router(y, g, shift, scale, w_router, b_router)
    return moe_experts(h, comb, w_gu, w_d, y, gate)


def kernel(x_prompt, x_sample, cache_k, cache_v, cache_logf, state_ssm, state_ssm_conv, state_ccv,
           page_table, c_prompt, c_sample, ada_w, ada_b, norm_g, ab_w_in, ssm_conv_w, ssm_conv_b,
           ssm_dt_bias, ssm_a_log, ssm_d, ssm_norm_g, ccv_w, ccv_b, ccv_ln_g, ccv_ln_b, ab_w_out,
           fox_w_in, fox_b_f, fox_w_out, moe_w_group, moe_b_group, moe_w_expert, moe_b_expert,
           moe_w_gate, moe_w_up, moe_w_down, final_g):
    d = D_MODEL
    bp, seq, _ = x_prompt.shape
    bs, dec_seq, _ = x_sample.shape
    depth = norm_g.shape[0]
    ts = bs * dec_seq

    n_c = bp + bs
    r_pad = -n_c % SUBLANES
    c_all = jnp.pad(jnp.concatenate([c_prompt, c_sample], axis=0), ((0, r_pad), (0, 0)))
    mod = ada_all(c_all, ada_w.reshape(depth * 2, d, 3 * d), ada_b.reshape(depth * 2, 1, 3 * d))

    yp = x_prompt
    ys = x_sample.reshape(1, ts, d)
    outs = {}
    for layer in range(depth):
        i = layer // 2
        (sh_p, sc_p, gt_p), (sh_s, sc_s, gt_s) = _mods(mod[2 * layer, :bp], mod[2 * layer, bp:n_c], dec_seq)
        g_mix = norm_g[layer, 0][None]
        if layer % 2 == 0:
            w_in = ab_w_in[i]
            o_dt = SSM_D_INNER + SSM_CONV_DIM
            o_u = o_dt + SSM_HEADS
            w_main = _hi_lo(jnp.concatenate([w_in[:, :o_dt], w_in[:, o_u:]], axis=1))
            w_dt = _hi_lo(_pad_lanes(w_in[:, o_dt:o_u]))
            widths = (SSM_D_INNER, SSM_CONV_DIM, 2 * CCV_DIM)
            ssm_w = (ssm_conv_w[i], ssm_conv_b[i][None], _pad_lanes(ssm_dt_bias[i][None]),
                     _pad_lanes(ssm_a_log[i][None]), jnp.repeat(ssm_d[i], SSM_HEAD_DIM)[None],
                     ssm_norm_g[i][None])
            ccv_p = (ccv_w[i], ccv_b[i][None], ccv_ln_g[i][None], ccv_ln_b[i][None])
            w_out = _hi_lo(ab_w_out[i])

            z, xbc, u, dt_raw = nm_matmul(yp, g_mix, sh_p, sc_p, w_main,
                                          [(w, BF16, None) for w in widths], w_aux=w_dt)
            y_ssm, st_p = ssd_mixer(xbc, z, dt_raw, *ssm_w,
                                    jnp.zeros((bp, SUBLANES, SSM_CONV_DIM), F32),
                                    jnp.zeros((bp, SSM_PAIRS, LANES, SSM_STATE), F32), seq, BF16)
            cv, ccv_new_p = ccv_mixer(u, *ccv_p, jnp.zeros((bp, CCV_HALO, CCV_DIM), F32), seq, BF16)
            yp = mm_residual([y_ssm, cv], w_out, yp, gt_p)
            outs["ssm_state_p"] = st_p.reshape(1, bp, SSM_HEADS, SSM_HEAD_DIM, SSM_STATE)
            outs["ssm_conv_p"] = xbc[:, seq - (SSM_CONV - 1):].astype(F32)[None]
            outs["ccv_p"] = ccv_new_p[:, CCV_HALO - (CCV_WIDTH - 1):][None]

            z, xbc, u, dt_raw = nm_matmul(ys, g_mix, sh_s, sc_s, w_main,
                                          [(w, F32, None) for w in widths], w_aux=w_dt)
            per_seq = lambda t: t.reshape(bs, dec_seq, t.shape[-1])
            pad_rows = lambda t, n: jnp.pad(t, ((0, 0), (0, n - t.shape[1]), (0, 0)))
            xbc_s = per_seq(xbc)
            conv_in = jnp.pad(state_ssm_conv[i], ((0, 0), (SUBLANES - (SSM_CONV - 1), 0), (0, 0)))
            y_ssm, st_s = ssd_mixer(pad_rows(xbc_s, SSM_CHUNK), pad_rows(per_seq(z), SSM_CHUNK),
                                    pad_rows(per_seq(dt_raw), SSM_CHUNK), *ssm_w, conv_in,
                                    state_ssm[i].reshape(bs, SSM_PAIRS, LANES, SSM_STATE), dec_seq, F32,
                                    precise=True)
            ccv_in = jnp.pad(state_ccv[i], ((0, 0), (CCV_HALO - (CCV_WIDTH - 1), 0), (0, 0)))
            cv, ccv_new_s = ccv_mixer(pad_rows(per_seq(u), SUBLANES), *ccv_p, ccv_in, dec_seq, F32)
            ys = mm_residual([y_ssm[:, :dec_seq].reshape(1, ts, SSM_D_INNER),
                              cv[:, :dec_seq].reshape(1, ts, CCV_DIM)], w_out, ys, gt_s)
            outs["ssm_state_s"] = st_s.reshape(1, bs, SSM_HEADS, SSM_HEAD_DIM, SSM_STATE)
            outs["ssm_conv_s"] = jnp.concatenate([state_ssm_conv[i], xbc_s], axis=1)[:, -(SSM_CONV - 1):][None]
            outs["ccv_s"] = ccv_new_s[:, CCV_HALO - (CCV_WIDTH - 1):][None]
        else:
            w_in = fox_w_in[i]
            w_main = _hi_lo(w_in[:, :3 * FOX_WIDTH])
            w_f = _hi_lo(_pad_lanes(w_in[:, 3 * FOX_WIDTH:]))
            b_f = _pad_lanes(fox_b_f[i][None])
            w_out = _hi_lo(fox_w_out[i])
            q_scale = FOX_HEAD_DIM ** -0.5

            q, k, v, fl = nm_matmul(yp, g_mix, sh_p, sc_p, w_main,
                                    [(FOX_WIDTH, BF16, q_scale), (FOX_WIDTH, F32, None),
                                     (FOX_WIDTH, F32, None)], w_aux=w_f)
            lf, cum, cum_t = logf_cumsum(fl, b_f)
            att = fox_prompt_attention(q, k, v, cum, cum_t.reshape(bp * LANES, 1, seq))
            yp = mm_residual([att], w_out, yp, gt_p)
            outs["k_p"] = k.reshape(1, bp, seq, FOX_HEADS, FOX_HEAD_DIM)
            outs["v_p"] = v.reshape(1, bp, seq, FOX_HEADS, FOX_HEAD_DIM)
            outs["lf_p"] = lf[:, :, :FOX_HEADS][None]

            q, k, v, fl = nm_matmul(ys, g_mix, sh_s, sc_s, w_main,
                                    [(FOX_WIDTH, F32, None)] * 3, w_aux=w_f)
            lf, _, _ = logf_cumsum(fl, b_f)
            lf_s = lf.reshape(bs, dec_seq, LANES)[:, :, :FOX_HEADS]
            n_new = dec_seq * FOX_HEADS
            head_rows = lambda t: t.reshape(bs, n_new, FOX_HEAD_DIM)
            pad_page = lambda t: jnp.pad(head_rows(t), ((0, 0), (0, LANES - n_new), (0, 0)))
            lf_new = jnp.pad(lf_s.reshape(bs, 1, n_new), ((0, 0), (0, SUBLANES - 1), (0, LANES - n_new)))
            n_pool = cache_k.shape[1]
            page_rows = PAGE_SIZE * FOX_HEADS
            att = fox_decode_attention(
                head_rows(q), pad_page(k), pad_page(v), lf_new,
                cache_k[i].reshape(n_pool, page_rows, FOX_HEAD_DIM),
                cache_v[i].reshape(n_pool, page_rows, FOX_HEAD_DIM),
                cache_logf[i].reshape(n_pool, page_rows // LANES, LANES), page_table)
            ys = mm_residual([att.reshape(1, ts, FOX_WIDTH)], w_out, ys, gt_s)
            outs["k_s"] = k.reshape(1, bs, dec_seq, FOX_HEADS, FOX_HEAD_DIM)
            outs["v_s"] = v.reshape(1, bs, dec_seq, FOX_HEADS, FOX_HEAD_DIM)
            outs["lf_s"] = lf_s[None]

        (sh_p, sc_p, gt_p), (sh_s, sc_s, gt_s) = _mods(mod[2 * layer + 1, :bp], mod[2 * layer + 1, bp:n_c],
                                                       dec_seq)
        g_moe = norm_g[layer, 1][None]
        w_router = _pad_lanes(jnp.concatenate([moe_w_expert[layer], moe_w_group[layer]], axis=1))
        b_router = _pad_lanes(jnp.concatenate([moe_b_expert[layer], moe_b_group[layer]])[None])
        w_gu = jnp.concatenate([moe_w_gate[layer], moe_w_up[layer]], axis=-1).astype(BF16)
        w_d = moe_w_down[layer].astype(BF16)
        yp = _moe_layer(yp, sh_p, sc_p, gt_p, g_moe, w_router, b_router, w_gu, w_d)
        ys = _moe_layer(ys, sh_s, sc_s, gt_s, g_moe, w_router, b_router, w_gu, w_d)

    y_prompt = final_rmsnorm(yp.reshape(bp * seq, d), final_g[None]).reshape(bp, seq, d)
    y_sample = final_rmsnorm(ys.reshape(ts, d), final_g[None]).reshape(bs, dec_seq, d)
    return (y_prompt, y_sample, outs["k_p"], outs["v_p"], outs["lf_p"], outs["k_s"], outs["v_s"],
            outs["lf_s"], outs["ssm_state_p"], outs["ssm_state_s"], outs["ssm_conv_p"],
            outs["ssm_conv_s"], outs["ccv_p"], outs["ccv_s"])
```

```python
import functools

import jax
import jax.numpy as jnp
from jax import lax
from jax.experimental import pallas as pl
from jax.experimental.pallas import tpu as pltpu

F32 = jnp.float32
BF16 = jnp.bfloat16

LANES = 128
SUBLANES = 8
VMEM_LIMIT_BYTES = 56 * 1024 * 1024

EPS = 1e-6
D_MODEL = 2048
SSM_HEAD_DIM = 64
SSM_HEADS = 32
SSM_GROUPS = 4
SSM_STATE = 128
SSM_CONV = 4
SSM_CHUNK = 128
SSM_D_INNER = SSM_HEADS * SSM_HEAD_DIM
SSM_BC = SSM_GROUPS * SSM_STATE
SSM_CONV_DIM = SSM_D_INNER + 2 * SSM_BC
SSM_PAIRS = SSM_HEADS // 2
PAIRS_PER_GROUP = SSM_PAIRS // SSM_GROUPS
CCV_DIM = 1024
CCV_WIDTH = 31
CCV_HALO = 32
FOX_HEADS = 16
FOX_HEAD_DIM = 128
FOX_WIDTH = FOX_HEADS * FOX_HEAD_DIM
PAGE_SIZE = 128
DECODE_PAGES_PER_STEP = 8
MOE_GROUPS = 4
MOE_EPG = 4
MOE_EXPERTS = 16
MOE_FF = 512
NEG_INF = float("-inf")


def _params(*sem):
    return pltpu.CompilerParams(dimension_semantics=sem, vmem_limit_bytes=VMEM_LIMIT_BYTES)


def _silu(x):
    return x * jax.nn.sigmoid(x)


def _softplus(x):
    return jnp.maximum(x, 0.0) + jnp.log1p(jnp.exp(-jnp.abs(x)))


def _dot(a, b):
    return jnp.dot(a, b, preferred_element_type=F32)


def _dot_nt(a, b):
    return lax.dot_general(a, b, (((1,), (1,)), ((), ())), preferred_element_type=F32)


def _split3(x):
    hi = x.astype(BF16)
    r1 = x - hi.astype(F32)
    mid = r1.astype(BF16)
    lo = (r1 - mid.astype(F32)).astype(BF16)
    return hi, mid, lo


def _split2(x):
    hi = x.astype(BF16)
    return hi, (x - hi.astype(F32)).astype(BF16)


def _dot_split2(a, b, dot=_dot):
    a_hi, a_lo = _split2(a)
    b_hi, b_lo = _split2(b)
    return dot(a_hi, b_hi) + (dot(a_lo, b_hi) + dot(a_hi, b_lo))


def _mask_dot(mask, x):
    mb = mask.astype(F32).astype(BF16)
    hi, mid, lo = _split3(x)
    return _dot(mb, hi) + _dot(mb, mid) + _dot(mb, lo)


def _dot_mask(x, mask):
    mb = mask.astype(F32).astype(BF16)
    hi, mid, lo = _split3(x)
    return _dot(hi, mb) + _dot(mid, mb) + _dot(lo, mb)


def _row_tile(n, pref):
    t = min(n, pref)
    while n % t:
        t //= 2
    return t


def _ada_kernel(c_ref, w_ref, b_ref, o_ref):
    o_ref[0] = _dot(_silu(c_ref[...]).astype(BF16), w_ref[0].astype(BF16)) + b_ref[0]


def ada_all(c_all, w, b):
    r, d = c_all.shape
    s, _, n = w.shape
    tn = 1024
    return pl.pallas_call(
        _ada_kernel,
        grid=(s, n // tn),
        in_specs=[pl.BlockSpec((r, d), lambda i, j: (0, 0)),
                  pl.BlockSpec((1, d, tn), lambda i, j: (i, 0, j)),
                  pl.BlockSpec((1, 1, tn), lambda i, j: (i, 0, j))],
        out_specs=pl.BlockSpec((1, r, tn), lambda i, j: (i, 0, j)),
        out_shape=jax.ShapeDtypeStruct((s, r, n), F32),
        compiler_params=_params("arbitrary", "arbitrary"),
        name="ada_all",
    )(c_all, w, b)


def _mod_spec(mod, tm, tn=None):
    if tn is None:
        d = mod.shape[-1]
        if mod.shape[1] == 1:
            return pl.BlockSpec((1, 1, d), lambda b, i, j: (b, 0, 0))
        return pl.BlockSpec((1, tm, d), lambda b, i, j: (b, i, 0))
    if mod.shape[1] == 1:
        return pl.BlockSpec((1, 1, tn), lambda b, i, j: (b, 0, j))
    return pl.BlockSpec((1, tm, tn), lambda b, i, j: (b, i, j))


def _norm_mod(x, g, shift, scale):
    ms = jnp.mean(x * x, axis=-1, keepdims=True)
    return (x * lax.rsqrt(ms + EPS) * g) * (1.0 + scale) + shift


def _nm_matmul_kernel(x_ref, g_ref, sh_ref, sc_ref, w_ref, *rest, ranges, scales, has_aux):
    n_out = len(ranges)
    if has_aux:
        waux_ref = rest[0]
        outs = rest[1:1 + n_out]
        aux_ref = rest[1 + n_out]
    else:
        outs = rest[:n_out]
    h_scr = rest[-1]
    j = pl.program_id(2)

    @pl.when(j == 0)
    def _():
        h = _norm_mod(x_ref[0], g_ref[...], sh_ref[0], sc_ref[0])
        h_scr[...] = h.astype(BF16)
        if has_aux:
            aux_ref[0] = _dot(h_scr[...], waux_ref[...])

    res = _dot(h_scr[...], w_ref[...])
    for (lo, hi), scale, o_ref in zip(ranges, scales, outs):
        @pl.when((j >= lo) & (j < hi))
        def _(o_ref=o_ref, scale=scale):
            val = res if scale is None else res * scale
            o_ref[0] = val.astype(o_ref.dtype)


def nm_matmul(x, g, shift, scale, w, outs, w_aux=None, tm_pref=512, tn=1024):
    bsz, l, d = x.shape
    n = w.shape[-1]
    tm = _row_tile(l, tm_pref)
    ranges, scales, out_shapes, out_specs = [], [], [], []
    lo = 0
    for width, dtype, sc in outs:
        nt = width // tn
        ranges.append((lo, lo + nt))
        scales.append(sc)
        out_shapes.append(jax.ShapeDtypeStruct((bsz, l, width), dtype))
        out_specs.append(pl.BlockSpec(
            (1, tm, tn), lambda b, i, j, lo=lo, nt=nt: (b, i, jnp.clip(j - lo, 0, nt - 1))))
        lo += nt
    assert lo * tn == n
    in_specs = [pl.BlockSpec((1, tm, d), lambda b, i, j: (b, i, 0)),
                pl.BlockSpec((1, d), lambda b, i, j: (0, 0)),
                _mod_spec(shift, tm), _mod_spec(scale, tm),
                pl.BlockSpec((d, tn), lambda b, i, j: (0, j))]
    args = [x, g, shift, scale, w]
    if w_aux is not None:
        in_specs.append(pl.BlockSpec((d, LANES), lambda b, i, j: (0, 0)))
        args.append(w_aux)
        out_shapes.append(jax.ShapeDtypeStruct((bsz, l, LANES), F32))
        out_specs.append(pl.BlockSpec((1, tm, LANES), lambda b, i, j: (b, i, 0)))
    return pl.pallas_call(
        functools.partial(_nm_matmul_kernel, ranges=tuple(ranges), scales=tuple(scales),
                          has_aux=w_aux is not None),
        grid=(bsz, l // tm, n // tn),
        in_specs=in_specs, out_specs=out_specs, out_shape=out_shapes,
        scratch_shapes=[pltpu.VMEM((tm, d), BF16)],
        compiler_params=_params("arbitrary", "arbitrary", "arbitrary"),
        name="nm_matmul",
    )(*args)


def _ssd_kernel(xbc_ref, z_ref, dt_ref, cw_ref, cb_ref, dtb_ref, alog_ref, dsk_ref, ng_ref,
                ic_ref, is_ref, y_ref, fs_ref, ext_scr, xc_scr, st_scr, y_scr, *, q, l_valid, precise):
    c = pl.program_id(1)
    nc = pl.num_programs(1)

    def mm(a, b, dot):
        if precise:
            return _dot_split2(a, b, dot)
        return dot(a.astype(BF16), b.astype(BF16))

    @pl.when(c == 0)
    def _():
        ext_scr[0:SUBLANES, :] = ic_ref[0]
        st_scr[...] = is_ref[0]

    @pl.when(c > 0)
    def _():
        ext_scr[0:SUBLANES, :] = ext_scr[q:q + SUBLANES, :]

    ext_scr[SUBLANES:SUBLANES + q, :] = xbc_ref[0].astype(F32)
    conv = cb_ref[...]
    for k in range(SSM_CONV):
        off = SUBLANES - (SSM_CONV - 1) + k
        conv = conv + cw_ref[k:k + 1, :] * ext_scr[off:off + q, :]
    xc_scr[...] = _silu(conv)

    row = lax.broadcasted_iota(jnp.int32, (q, LANES), 0)
    lane = lax.broadcasted_iota(jnp.int32, (q, LANES), 1)
    left = lane < SSM_HEAD_DIM
    dt = _softplus(dt_ref[0] + dtb_ref[...])
    dt = jnp.where(row + c * q < l_valid, dt, 0.0)
    a = dt * (-jnp.exp(alog_ref[...]))
    rq = lax.broadcasted_iota(jnp.int32, (q, q), 0)
    cq = lax.broadcasted_iota(jnp.int32, (q, q), 1)
    causal = rq >= cq
    a_cum = _mask_dot(causal, a)
    a_cum_t = a_cum.T
    a_last = a_cum[q - 1:q, :]

    for g in range(SSM_GROUPS):
        b_lo = SSM_D_INNER + g * SSM_STATE
        c_lo = SSM_D_INNER + SSM_BC + g * SSM_STATE
        bm = xc_scr[:, b_lo:b_lo + SSM_STATE]
        cm = xc_scr[:, c_lo:c_lo + SSM_STATE]
        if not precise:
            bm, cm = bm.astype(BF16), cm.astype(BF16)
        cb = mm(cm, bm, _dot_nt)
        for pp in range(PAIRS_PER_GROUP):
            p = g * PAIRS_PER_GROUP + pp
            h0, h1 = 2 * p, 2 * p + 1
            col0, col1 = a_cum[:, h0:h0 + 1], a_cum[:, h1:h1 + 1]
            l0 = jnp.exp(jnp.where(causal, col0 - a_cum_t[h0:h0 + 1, :], NEG_INF))
            l1 = jnp.exp(jnp.where(causal, col1 - a_cum_t[h1:h1 + 1, :], NEG_INF))
            m = jnp.concatenate([cb * l0, cb * l1], axis=1)
            xs = xc_scr[:, p * LANES:(p + 1) * LANES]
            xd = xs * jnp.where(left, dt[:, h0:h0 + 1], dt[:, h1:h1 + 1])
            xd_bd = jnp.concatenate([jnp.where(left, xd, 0.0), jnp.where(left, 0.0, xd)],
                                    axis=0)
            y = mm(m, xd_bd, _dot)
            st = st_scr[p]
            y = y + mm(cm, st, _dot_nt) * jnp.exp(jnp.where(left, col0, col1))
            y_scr[:, p * LANES:(p + 1) * LANES] = y + dsk_ref[:, p * LANES:(p + 1) * LANES] * xs
            al0, al1 = a_last[:, h0:h0 + 1], a_last[:, h1:h1 + 1]
            dte = jnp.exp(jnp.where(left, al0 - col0, al1 - col1))
            upd = mm((xd * dte).T, bm, _dot)
            half = (SSM_HEAD_DIM, SSM_STATE)
            decay = jnp.exp(jnp.concatenate([jnp.broadcast_to(al0, half), jnp.broadcast_to(al1, half)],
                                            axis=0))
            st_scr[p] = st * decay + upd

    gw = SSM_D_INNER // SSM_GROUPS
    for g in range(SSM_GROUPS):
        sl = slice(g * gw, (g + 1) * gw)
        v = y_scr[:, sl] * _silu(z_ref[0, :, sl].astype(F32))
        v = v * lax.rsqrt(jnp.mean(v * v, axis=-1, keepdims=True) + EPS)
        y_ref[0, :, sl] = (v * ng_ref[:, sl]).astype(y_ref.dtype)

    @pl.when(c == nc - 1)
    def _():
        fs_ref[0] = st_scr[...]


def ssd_mixer(xbc, z, dt_raw, conv_w, conv_b, dt_bias, a_log, d_skip, norm_g, init_conv, init_state,
              l_valid, out_dtype, precise=False):
    bsz, l, _ = xbc.shape
    q = SSM_CHUNK
    assert l % q == 0
    vec = lambda n: pl.BlockSpec((1, n), lambda b, c: (0, 0))
    state_spec = pl.BlockSpec((1, SSM_PAIRS, LANES, SSM_STATE), lambda b, c: (b, 0, 0, 0))
    return pl.pallas_call(
        functools.partial(_ssd_kernel, q=q, l_valid=l_valid, precise=precise),
        grid=(bsz, l // q),
        in_specs=[pl.BlockSpec((1, q, SSM_CONV_DIM), lambda b, c: (b, c, 0)),
                  pl.BlockSpec((1, q, SSM_D_INNER), lambda b, c: (b, c, 0)),
                  pl.BlockSpec((1, q, LANES), lambda b, c: (b, c, 0)),
                  pl.BlockSpec((SSM_CONV, SSM_CONV_DIM), lambda b, c: (0, 0)),
                  vec(SSM_CONV_DIM), vec(LANES), vec(LANES), vec(SSM_D_INNER), vec(SSM_D_INNER),
                  pl.BlockSpec((1, SUBLANES, SSM_CONV_DIM), lambda b, c: (b, 0, 0)),
                  state_spec],
        out_specs=[pl.BlockSpec((1, q, SSM_D_INNER), lambda b, c: (b, c, 0)), state_spec],
        out_shape=[jax.ShapeDtypeStruct((bsz, l, SSM_D_INNER), out_dtype),
                   jax.ShapeDtypeStruct((bsz, SSM_PAIRS, LANES, SSM_STATE), F32)],
        scratch_shapes=[pltpu.VMEM((q + SUBLANES, SSM_CONV_DIM), F32),
                        pltpu.VMEM((q, SSM_CONV_DIM), F32),
                        pltpu.VMEM((SSM_PAIRS, LANES, SSM_STATE), F32),
                        pltpu.VMEM((q, SSM_D_INNER), F32)],
        compiler_params=_params("arbitrary", "arbitrary"),
        name="ssd_mixer",
    )(xbc, z, dt_raw, conv_w, conv_b, dt_bias, a_log, d_skip, norm_g, init_conv, init_state)


def _ccv_kernel(u_ref, w_ref, b_ref, lg_ref, lb_ref, init_ref, o_ref, new_ref, ext_scr, cv_scr,
                win_scr, *, tl, lv_last, rb):
    t = pl.program_id(1)
    nt = pl.num_programs(1)

    @pl.when(t == 0)
    def _():
        ext_scr[0:CCV_HALO, :] = init_ref[0]

    @pl.when(t > 0)
    def _():
        ext_scr[0:CCV_HALO, :] = ext_scr[tl:tl + CCV_HALO, :]

    ua = u_ref[0, :, 0:CCV_DIM].astype(F32)
    ug = u_ref[0, :, CCV_DIM:2 * CCV_DIM].astype(F32)
    ext_scr[CCV_HALO:CCV_HALO + tl, :] = ua * jax.nn.sigmoid(ug)

    first = CCV_HALO - (CCV_WIDTH - 1)

    def body(r, carry):
        r0 = pl.multiple_of(r * rb, rb)
        win_scr[...] = ext_scr[pl.ds(r0, rb + CCV_HALO), :]
        acc = jnp.broadcast_to(b_ref[...], (rb, CCV_DIM))
        for k in range(CCV_WIDTH):
            acc = acc + w_ref[k:k + 1, :] * win_scr[first + k:first + k + rb, :]
        cv_scr[pl.ds(r0, rb), :] = acc
        return carry

    lax.fori_loop(0, tl // rb, body, 0)
    cv = cv_scr[...]
    mu = jnp.mean(cv, axis=-1, keepdims=True)
    var = jnp.mean(jnp.square(cv - mu), axis=-1, keepdims=True)
    y = (cv - mu) * lax.rsqrt(var + EPS) * lg_ref[...] + lb_ref[...]
    o_ref[0] = _silu(y).astype(o_ref.dtype)

    @pl.when(t == nt - 1)
    def _():
        new_ref[0] = ext_scr[lv_last:lv_last + CCV_HALO, :]


def ccv_mixer(u, w, b, ln_g, ln_b, init, l_valid, out_dtype):
    bsz, l, _ = u.shape
    tl = _row_tile(l, 512)
    lv_last = l_valid - (l - tl)
    rb = min(tl, 32)
    assert 0 < lv_last <= tl and (tl >= CCV_HALO or l == tl)
    vec = lambda: pl.BlockSpec((1, CCV_DIM), lambda bb, t: (0, 0))
    return pl.pallas_call(
        functools.partial(_ccv_kernel, tl=tl, lv_last=lv_last, rb=rb),
        grid=(bsz, l // tl),
        in_specs=[pl.BlockSpec((1, tl, 2 * CCV_DIM), lambda bb, t: (bb, t, 0)),
                  pl.BlockSpec((CCV_WIDTH, CCV_DIM), lambda bb, t: (0, 0)),
                  vec(), vec(), vec(),
                  pl.BlockSpec((1, CCV_HALO, CCV_DIM), lambda bb, t: (bb, 0, 0))],
        out_specs=[pl.BlockSpec((1, tl, CCV_DIM), lambda bb, t: (bb, t, 0)),
                   pl.BlockSpec((1, CCV_HALO, CCV_DIM), lambda bb, t: (bb, 0, 0))],
        out_shape=[jax.ShapeDtypeStruct((bsz, l, CCV_DIM), out_dtype),
                   jax.ShapeDtypeStruct((bsz, CCV_HALO, CCV_DIM), F32)],
        scratch_shapes=[pltpu.VMEM((tl + 2 * CCV_HALO, CCV_DIM), F32),
                        pltpu.VMEM((tl, CCV_DIM), F32),
                        pltpu.VMEM((rb + CCV_HALO, CCV_DIM), F32)],
        compiler_params=_params("arbitrary", "arbitrary"),
        name="ccv_mixer",
    )(u, w, b, ln_g, ln_b, init)


def _mm_res_kernel(*refs, n_lhs):
    a_refs = refs[:n_lhs]
    w_refs = refs[n_lhs:2 * n_lhs]
    res_ref, gate_ref, o_ref = refs[2 * n_lhs:]

    acc = _dot(a_refs[0][0].astype(BF16), w_refs[0][...])
    for a_ref, w_ref in zip(a_refs[1:], w_refs[1:]):
        acc = acc + _dot(a_ref[0].astype(BF16), w_ref[...])
    o_ref[0] = res_ref[0] + gate_ref[0] * acc


def mm_residual(lhs, w, res, gate, tm_pref=1024, tn=1024):
    bsz, l, n = res.shape
    tm = _row_tile(l, tm_pref)
    in_specs, w_specs = [], []
    row = 0
    for a in lhs:
        k = a.shape[-1]
        assert row % k == 0
        in_specs.append(pl.BlockSpec((1, tm, k), lambda b, i, j: (b, i, 0)))
        w_specs.append(pl.BlockSpec((k, tn), lambda b, i, j, rb=row // k: (rb, j)))
        row += k
    assert row == w.shape[0]
    in_specs += w_specs + [pl.BlockSpec((1, tm, tn), lambda b, i, j: (b, i, j)), _mod_spec(gate, tm, tn)]
    return pl.pallas_call(
        functools.partial(_mm_res_kernel, n_lhs=len(lhs)),
        grid=(bsz, l // tm, n // tn),
        in_specs=in_specs,
        out_specs=pl.BlockSpec((1, tm, tn), lambda b, i, j: (b, i, j)),
        out_shape=jax.ShapeDtypeStruct((bsz, l, n), F32),
        compiler_params=_params("arbitrary", "arbitrary", "arbitrary"),
        name="mm_residual",
    )(*lhs, *([w] * len(lhs)), res, gate)


def _router_kernel(x_ref, g_ref, sh_ref, sc_ref, wr_ref, br_ref, h_ref, comb_ref):
    h = _norm_mod(x_ref[0], g_ref[...], sh_ref[0], sc_ref[0])
    h_ref[0] = h.astype(h_ref.dtype)
    logits = _dot(h.astype(BF16), wr_ref[...].astype(BF16)) + br_ref[...]
    tm = logits.shape[0]
    lane = lax.broadcasted_iota(jnp.int32, (tm, LANES), 1).astype(F32)
    first_of = lambda hit: jnp.min(jnp.where(hit, lane, float(LANES)), axis=-1, keepdims=True)
    is_group = (lane >= MOE_EXPERTS) & (lane < MOE_EXPERTS + MOE_GROUPS)
    gl = jnp.where(is_group, logits, NEG_INF)
    gmax = jnp.max(gl, axis=-1, keepdims=True)
    gsum = jnp.sum(jnp.exp(gl - gmax), axis=-1, keepdims=True)
    g_val = 1.0 / gsum
    g_idx = first_of(gl == gmax) - MOE_EXPERTS
    in_group = (lane >= g_idx * MOE_EPG) & (lane < (g_idx + 1.0) * MOE_EPG)
    e1 = jnp.where(in_group, logits, NEG_INF)
    m1 = jnp.max(e1, axis=-1, keepdims=True)
    i1 = first_of(e1 == m1)
    e2 = jnp.where(lane == i1, NEG_INF, e1)
    m2 = jnp.max(e2, axis=-1, keepdims=True)
    i2 = first_of(e2 == m2)
    r = jnp.exp(m2 - m1)
    w1 = g_val / (1.0 + r)
    w2 = g_val * r / (1.0 + r)
    comb_ref[0] = jnp.where(lane == i1, w1, 0.0) + jnp.where(lane == i2, w2, 0.0)


def moe_router(x, g, shift, scale, w_router, b_router, tm_pref=512):
    bsz, l, d = x.shape
    tm = _row_tile(l, tm_pref)
    return pl.pallas_call(
        _router_kernel,
        grid=(bsz, l // tm, 1),
        in_specs=[pl.BlockSpec((1, tm, d), lambda b, i, j: (b, i, 0)),
                  pl.BlockSpec((1, d), lambda b, i, j: (0, 0)),
                  _mod_spec(shift, tm), _mod_spec(scale, tm),
                  pl.BlockSpec((d, LANES), lambda b, i, j: (0, 0)),
                  pl.BlockSpec((1, LANES), lambda b, i, j: (0, 0))],
        out_specs=[pl.BlockSpec((1, tm, d), lambda b, i, j: (b, i, 0)),
                   pl.BlockSpec((1, tm, LANES), lambda b, i, j: (b, i, 0))],
        out_shape=[jax.ShapeDtypeStruct((bsz, l, d), BF16),
                   jax.ShapeDtypeStruct((bsz, l, LANES), F32)],
        compiler_params=_params("arbitrary", "arbitrary", "arbitrary"),
        name="moe_router",
    )(x, g, shift, scale, w_router, b_router)


def _moe_kernel(h_ref, comb_ref, wgu_ref, wd_ref, res_ref, gate_ref, o_ref, acc_scr):
    e = pl.program_id(2)

    @pl.when(e == 0)
    def _():
        acc_scr[...] = jnp.zeros_like(acc_scr)

    comb = comb_ref[0]
    lane = lax.broadcasted_iota(jnp.int32, comb.shape, 1)
    col = jnp.sum(jnp.where(lane == e, comb, 0.0), axis=-1, keepdims=True)

    @pl.when(jnp.max(jnp.abs(col)) > 0.0)
    def _():
        gu = _dot(h_ref[0], wgu_ref[0])
        act = _silu(gu[:, :MOE_FF]) * gu[:, MOE_FF:] * col
        acc_scr[...] += _dot(act.astype(BF16), wd_ref[0])

    @pl.when(e == pl.num_programs(2) - 1)
    def _():
        o_ref[0] = res_ref[0] + gate_ref[0] * acc_scr[...]


def moe_experts(h, comb, w_gu, w_d, res, gate, tm_pref=512):
    bsz, l, d = h.shape
    tm = _row_tile(l, tm_pref)
    e = w_gu.shape[0]
    if gate.shape[1] == 1:
        gate_spec = pl.BlockSpec((1, 1, d), lambda b, i, j: (b, 0, 0))
    else:
        gate_spec = pl.BlockSpec((1, tm, d), lambda b, i, j: (b, i, 0))
    return pl.pallas_call(
        _moe_kernel,
        grid=(bsz, l // tm, e),
        in_specs=[pl.BlockSpec((1, tm, d), lambda b, i, j: (b, i, 0)),
                  pl.BlockSpec((1, tm, LANES), lambda b, i, j: (b, i, 0)),
                  pl.BlockSpec((1, d, 2 * MOE_FF), lambda b, i, j: (j, 0, 0)),
                  pl.BlockSpec((1, MOE_FF, d), lambda b, i, j: (j, 0, 0)),
                  pl.BlockSpec((1, tm, d), lambda b, i, j: (b, i, 0)),
                  gate_spec],
        out_specs=pl.BlockSpec((1, tm, d), lambda b, i, j: (b, i, 0)),
        out_shape=jax.ShapeDtypeStruct((bsz, l, d), F32),
        scratch_shapes=[pltpu.VMEM((tm, d), F32)],
        compiler_params=_params("arbitrary", "arbitrary", "arbitrary"),
        name="moe_experts",
    )(h, comb, w_gu, w_d, res, gate)


def _logf_kernel(fl_ref, bf_ref, lf_ref, cum_ref, cumt_ref, carry_scr, *, tc):
    c = pl.program_id(1)

    @pl.when(c == 0)
    def _():
        carry_scr[...] = jnp.zeros_like(carry_scr)

    x = fl_ref[0] + bf_ref[...]
    lf = jnp.minimum(x, 0.0) - jnp.log1p(jnp.exp(-jnp.abs(x)))
    lf_ref[0] = lf
    r = lax.broadcasted_iota(jnp.int32, (tc, tc), 0)
    cc = lax.broadcasted_iota(jnp.int32, (tc, tc), 1)
    cum = _mask_dot(r >= cc, lf) + carry_scr[...]
    cum_ref[0] = cum
    cumt_ref[0] = cum.T
    carry_scr[...] = cum[tc - 1:tc, :]


def logf_cumsum(fl, b_f):
    bsz, l, _ = fl.shape
    tc = _row_tile(l, 256)
    return pl.pallas_call(
        functools.partial(_logf_kernel, tc=tc),
        grid=(bsz, l // tc),
        in_specs=[pl.BlockSpec((1, tc, LANES), lambda b, c: (b, c, 0)),
                  pl.BlockSpec((1, LANES), lambda b, c: (0, 0))],
        out_specs=[pl.BlockSpec((1, tc, LANES), lambda b, c: (b, c, 0)),
                   pl.BlockSpec((1, tc, LANES), lambda b, c: (b, c, 0)),
                   pl.BlockSpec((1, LANES, tc), lambda b, c: (b, 0, c))],
        out_shape=[jax.ShapeDtypeStruct((bsz, l, LANES), F32),
                   jax.ShapeDtypeStruct((bsz, l, LANES), F32),
                   jax.ShapeDtypeStruct((bsz, LANES, l), F32)],
        scratch_shapes=[pltpu.VMEM((1, LANES), F32)],
        compiler_params=_params("arbitrary", "arbitrary"),
        name="logf_cumsum",
    )(fl, b_f)


def _flash_kernel(qt_ref, kt_ref, q_ref, k_ref, v_ref, fc_ref, fr_ref, o_ref, m_scr, l_scr, acc_scr, fq_scr):
    h = pl.program_id(1)
    i = qt_ref[pl.program_id(2)]
    j = kt_ref[pl.program_id(2)]

    @pl.when(j == 0)
    def _():
        m_scr[...] = jnp.full_like(m_scr, NEG_INF)
        l_scr[...] = jnp.zeros_like(l_scr)
        acc_scr[...] = jnp.zeros_like(acc_scr)
        fc = fc_ref[0]
        lane = lax.broadcasted_iota(jnp.int32, fc.shape, 1)
        fq_scr[...] = jnp.sum(jnp.where(lane == h, fc, 0.0), axis=-1, keepdims=True)

    def step(on_diagonal):
        s = _dot_nt(q_ref[0], k_ref[0].astype(BF16))
        s = s + (fq_scr[...] - fr_ref[0])
        if on_diagonal:
            row = lax.broadcasted_iota(jnp.int32, s.shape, 0)
            col = lax.broadcasted_iota(jnp.int32, s.shape, 1)
            s = jnp.where(col <= row, s, NEG_INF)
        m_old = m_scr[...]
        m_new = jnp.maximum(m_old, jnp.max(s, axis=-1, keepdims=True))
        alpha = jnp.exp(m_old - m_new)
        p = jnp.exp(s - m_new)
        l_scr[...] = alpha * l_scr[...] + jnp.sum(p, axis=-1, keepdims=True)
        acc_scr[...] = alpha * acc_scr[...] + _dot(p.astype(BF16), v_ref[0].astype(BF16))
        m_scr[...] = m_new

    @pl.when(j < i)
    def _():
        step(False)

    @pl.when(j == i)
    def _():
        step(True)
        o_ref[0] = (acc_scr[...] / l_scr[...]).astype(o_ref.dtype)


def fox_prompt_attention(q, k, v, cum, cum_t):
    bsz, l, _ = q.shape
    t = _row_tile(l, 1024)
    n = l // t
    pairs = [(i, j) for i in range(n) for j in range(i + 1)]
    q_tile = jnp.array([i for i, _ in pairs], jnp.int32)
    k_tile = jnp.array([j for _, j in pairs], jnp.int32)
    q_spec = pl.BlockSpec((1, t, FOX_HEAD_DIM), lambda b, h, p, qt, kt: (b, qt[p], h))
    kv_spec = pl.BlockSpec((1, t, FOX_HEAD_DIM), lambda b, h, p, qt, kt: (b, kt[p], h))
    grid_spec = pltpu.PrefetchScalarGridSpec(
        num_scalar_prefetch=2,
        grid=(bsz, FOX_HEADS, len(pairs)),
        in_specs=[q_spec, kv_spec, kv_spec,
                  pl.BlockSpec((1, t, LANES), lambda b, h, p, qt, kt: (b, qt[p], 0)),
                  pl.BlockSpec((1, 1, t), lambda b, h, p, qt, kt: (b * LANES + h, 0, kt[p]))],
        out_specs=q_spec,
        scratch_shapes=[pltpu.VMEM((t, 1), F32), pltpu.VMEM((t, 1), F32),
                        pltpu.VMEM((t, FOX_HEAD_DIM), F32), pltpu.VMEM((t, 1), F32)])
    return pl.pallas_call(
        _flash_kernel,
        grid_spec=grid_spec,
        out_shape=jax.ShapeDtypeStruct((bsz, l, FOX_WIDTH), BF16),
        compiler_params=_params("arbitrary", "arbitrary", "arbitrary"),
        name="fox_prompt_attention",
    )(q_tile, k_tile, q, k, v, cum, cum_t)


def _decode_kernel(pt_ref, q_ref, kn_ref, vn_ref, lfn_ref, *rest, n_pp):
    kc_refs, vc_refs, lfc_refs = rest[:n_pp], rest[n_pp:2 * n_pp], rest[2 * n_pp:3 * n_pp]
    o_ref, qb_scr, m_scr, l_scr, acc_scr, carry_scr = rest[3 * n_pp:]
    j = pl.program_id(1)
    n_rows = q_ref.shape[1]
    li = lax.broadcasted_iota(jnp.int32, (LANES, LANES), 0)
    lj = lax.broadcasted_iota(jnp.int32, (LANES, LANES), 1)
    head_mask = FOX_HEADS - 1
    same_head = (li & head_mask) == (lj & head_mask)
    row = lax.broadcasted_iota(jnp.int32, (n_rows, LANES), 0)
    lane = lax.broadcasted_iota(jnp.int32, (n_rows, LANES), 1)
    own = (row & head_mask) == (lane & head_mask)

    def process(ks, vs, lf, is_new):
        per_page = ks[0].shape[0] // LANES
        n_blk = per_page * len(ks)
        qb = qb_scr[...]
        scores = [_dot_nt(qb, k.astype(BF16)) for k in ks]
        within = _dot_mask(lf, same_head & (li > lj))
        tot = _dot_mask(lf, same_head)
        after = within + carry_scr[...]
        if n_blk > 1:
            ri = lax.broadcasted_iota(jnp.int32, (n_blk, n_blk), 0)
            rj = lax.broadcasted_iota(jnp.int32, (n_blk, n_blk), 1)
            after = after + _mask_dot(rj > ri, tot)
        valid = own & (lane <= row) if is_new else own
        blocks = []
        for r in range(n_blk):
            s_blk = scores[r // per_page][:, (r % per_page) * LANES:(r % per_page + 1) * LANES]
            blocks.append(jnp.where(valid, s_blk + after[r:r + 1, :], NEG_INF))
        blk_max = blocks[0]
        for blk in blocks[1:]:
            blk_max = jnp.maximum(blk_max, blk)
        m_old = m_scr[...]
        m_new = jnp.maximum(m_old, jnp.max(blk_max, axis=-1, keepdims=True))
        alpha = jnp.exp(m_old - m_new)
        probs = [jnp.exp(blk - m_new) for blk in blocks]
        blk_sum = probs[0]
        for pr in probs[1:]:
            blk_sum = blk_sum + pr
        l_scr[...] = alpha * l_scr[...] + jnp.sum(blk_sum, axis=-1, keepdims=True)
        acc = alpha * acc_scr[...]
        for i, v in enumerate(vs):
            p = jnp.concatenate([pr.astype(BF16) for pr in probs[i * per_page:(i + 1) * per_page]], axis=1)
            acc = acc + _dot(p, v.astype(BF16))
        acc_scr[...] = acc
        m_scr[...] = m_new
        carry_scr[...] = carry_scr[...] + jnp.sum(tot, axis=0, keepdims=True)

    @pl.when(j == 0)
    def _():
        m_scr[...] = jnp.full_like(m_scr, NEG_INF)
        l_scr[...] = jnp.zeros_like(l_scr)
        acc_scr[...] = jnp.zeros_like(acc_scr)
        carry_scr[...] = jnp.zeros_like(carry_scr)
        qb_scr[...] = (q_ref[0] * (FOX_HEAD_DIM ** -0.5)).astype(BF16)
        process([kn_ref[0]], [vn_ref[0]], lfn_ref[0], True)

    @pl.when(j > 0)
    def _():
        process([r[...] for r in kc_refs], [r[...] for r in vc_refs],
                jnp.concatenate([r[...] for r in lfc_refs], axis=0), False)

    @pl.when(j == pl.num_programs(1) - 1)
    def _():
        o_ref[0] = acc_scr[...] / l_scr[...]


def fox_decode_attention(q, k_new, v_new, lf_new, cache_k, cache_v, cache_lf, page_table):
    bsz, n_rows, _ = q.shape
    n_pages = page_table.shape[1]
    page_rows = PAGE_SIZE * FOX_HEADS
    n_pp = DECODE_PAGES_PER_STEP
    assert n_pages % n_pp == 0

    def per_page(rows, slot):
        def index(b, j, pt):
            return pt[b * n_pages + n_pages - jnp.maximum(j, 1) * n_pp + slot], 0, 0
        return pl.BlockSpec((None, rows, LANES), index)

    per_seq = lambda rows: pl.BlockSpec((1, rows, LANES), lambda b, j, pt: (b, 0, 0))
    slots = range(n_pp)
    grid_spec = pltpu.PrefetchScalarGridSpec(
        num_scalar_prefetch=1,
        grid=(bsz, n_pages // n_pp + 1),
        in_specs=[per_seq(n_rows), per_seq(LANES), per_seq(LANES), per_seq(SUBLANES)]
        + [per_page(page_rows, s) for s in slots] + [per_page(page_rows, s) for s in slots]
        + [per_page(page_rows // LANES, s) for s in slots],
        out_specs=per_seq(n_rows),
        scratch_shapes=[pltpu.VMEM((n_rows, LANES), BF16),
                        pltpu.VMEM((n_rows, 1), F32),
                        pltpu.VMEM((n_rows, 1), F32),
                        pltpu.VMEM((n_rows, LANES), F32),
                        pltpu.VMEM((1, LANES), F32)])
    return pl.pallas_call(
        functools.partial(_decode_kernel, n_pp=n_pp),
        grid_spec=grid_spec,
        out_shape=jax.ShapeDtypeStruct((bsz, n_rows, LANES), F32),
        compiler_params=_params("arbitrary", "arbitrary"),
        name="fox_decode_attention",
    )(page_table.reshape(-1), q, k_new, v_new, lf_new, *([cache_k] * n_pp), *([cache_v] * n_pp),
      *([cache_lf] * n_pp))


def _rms_kernel(x_ref, g_ref, o_ref):
    x = x_ref[...]
    o_ref[...] = x * lax.rsqrt(jnp.mean(x * x, axis=-1, keepdims=True) + EPS) * g_ref[...]


def final_rmsnorm(x, g):
    t, d = x.shape
    tm = _row_tile(t, 1024)
    return pl.pallas_call(
        _rms_kernel,
        grid=(t // tm,),
        in_specs=[pl.BlockSpec((tm, d), lambda i: (i, 0)), pl.BlockSpec((1, d), lambda i: (0, 0))],
        out_specs=pl.BlockSpec((tm, d), lambda i: (i, 0)),
        out_shape=jax.ShapeDtypeStruct((t, d), F32),
        compiler_params=_params("arbitrary"),
        name="final_rmsnorm",
    )(x, g)


def _pad_lanes(v, width=LANES):
    return jnp.pad(v, [(0, 0)] * (v.ndim - 1) + [(0, width - v.shape[-1])])


def _mods(mod_p, mod_s, dec_seq):
    d = D_MODEL
    parts_p = [mod_p[:, None, k * d:(k + 1) * d] for k in range(3)]
    parts_s = [jnp.repeat(mod_s[:, k * d:(k + 1) * d], dec_seq, axis=0)[None] for k in range(3)]
    return parts_p, parts_s


def _moe_layer(y, shift, scale, gate, g, w_router, b_router, w_gu, w_d):
    h, comb = moe_router(y, g, shift, scale, w_router, b_router)
    return moe_experts(h, comb, w_gu, w_d, y, gate)


def kernel(x_prompt, x_sample, cache_k, cache_v, cache_logf, state_ssm, state_ssm_conv, state_ccv,
           page_table, c_prompt, c_sample, ada_w, ada_b, norm_g, ab_w_in, ssm_conv_w, ssm_conv_b,
           ssm_dt_bias, ssm_a_log, ssm_d, ssm_norm_g, ccv_w, ccv_b, ccv_ln_g, ccv_ln_b, ab_w_out,
           fox_w_in, fox_b_f, fox_w_out, moe_w_group, moe_b_group, moe_w_expert, moe_b_expert,
           moe_w_gate, moe_w_up, moe_w_down, final_g):
    d = D_MODEL
    bp, seq, _ = x_prompt.shape
    bs, dec_seq, _ = x_sample.shape
    depth = norm_g.shape[0]
    ts = bs * dec_seq

    n_c = bp + bs
    r_pad = -n_c % SUBLANES
    c_all = jnp.pad(jnp.concatenate([c_prompt, c_sample], axis=0), ((0, r_pad), (0, 0)))
    mod = ada_all(c_all, ada_w.reshape(depth * 2, d, 3 * d), ada_b.reshape(depth * 2, 1, 3 * d))

    yp = x_prompt
    ys = x_sample.reshape(1, ts, d)
    outs = {}
    for layer in range(depth):
        i = layer // 2
        (sh_p, sc_p, gt_p), (sh_s, sc_s, gt_s) = _mods(mod[2 * layer, :bp], mod[2 * layer, bp:n_c], dec_seq)
        g_mix = norm_g[layer, 0][None]
        if layer % 2 == 0:
            w_in = ab_w_in[i]
            o_dt = SSM_D_INNER + SSM_CONV_DIM
            o_u = o_dt + SSM_HEADS
            w_main = jnp.concatenate([w_in[:, :o_dt], w_in[:, o_u:]], axis=1).astype(BF16)
            w_dt = _pad_lanes(w_in[:, o_dt:o_u]).astype(BF16)
            widths = (SSM_D_INNER, SSM_CONV_DIM, 2 * CCV_DIM)
            ssm_w = (ssm_conv_w[i], ssm_conv_b[i][None], _pad_lanes(ssm_dt_bias[i][None]),
                     _pad_lanes(ssm_a_log[i][None]), jnp.repeat(ssm_d[i], SSM_HEAD_DIM)[None],
                     ssm_norm_g[i][None])
            ccv_p = (ccv_w[i], ccv_b[i][None], ccv_ln_g[i][None], ccv_ln_b[i][None])
            w_out = ab_w_out[i].astype(BF16)

            z, xbc, u, dt_raw = nm_matmul(yp, g_mix, sh_p, sc_p, w_main,
                                          [(w, BF16, None) for w in widths], w_aux=w_dt)
            y_ssm, st_p = ssd_mixer(xbc, z, dt_raw, *ssm_w,
                                    jnp.zeros((bp, SUBLANES, SSM_CONV_DIM), F32),
                                    jnp.zeros((bp, SSM_PAIRS, LANES, SSM_STATE), F32), seq, BF16)
            cv, ccv_new_p = ccv_mixer(u, *ccv_p, jnp.zeros((bp, CCV_HALO, CCV_DIM), F32), seq, BF16)
            yp = mm_residual([y_ssm, cv], w_out, yp, gt_p)
            outs["ssm_state_p"] = st_p.reshape(1, bp, SSM_HEADS, SSM_HEAD_DIM, SSM_STATE)
            outs["ssm_conv_p"] = xbc[:, seq - (SSM_CONV - 1):].astype(F32)[None]
            outs["ccv_p"] = ccv_new_p[:, CCV_HALO - (CCV_WIDTH - 1):][None]

            z, xbc, u, dt_raw = nm_matmul(ys, g_mix, sh_s, sc_s, w_main,
                                          [(w, F32, None) for w in widths], w_aux=w_dt)
            per_seq = lambda t: t.reshape(bs, dec_seq, t.shape[-1])
            pad_rows = lambda t, n: jnp.pad(t, ((0, 0), (0, n - t.shape[1]), (0, 0)))
            xbc_s = per_seq(xbc)
            conv_in = jnp.pad(state_ssm_conv[i], ((0, 0), (SUBLANES - (SSM_CONV - 1), 0), (0, 0)))
            y_ssm, st_s = ssd_mixer(pad_rows(xbc_s, SSM_CHUNK), pad_rows(per_seq(z), SSM_CHUNK),
                                    pad_rows(per_seq(dt_raw), SSM_CHUNK), *ssm_w, conv_in,
                                    state_ssm[i].reshape(bs, SSM_PAIRS, LANES, SSM_STATE), dec_seq, F32,
                                    precise=True)
            ccv_in = jnp.pad(state_ccv[i], ((0, 0), (CCV_HALO - (CCV_WIDTH - 1), 0), (0, 0)))
            cv, ccv_new_s = ccv_mixer(pad_rows(per_seq(u), SUBLANES), *ccv_p, ccv_in, dec_seq, F32)
            ys = mm_residual([y_ssm[:, :dec_seq].reshape(1, ts, SSM_D_INNER),
                              cv[:, :dec_seq].reshape(1, ts, CCV_DIM)], w_out, ys, gt_s)
            outs["ssm_state_s"] = st_s.reshape(1, bs, SSM_HEADS, SSM_HEAD_DIM, SSM_STATE)
            outs["ssm_conv_s"] = jnp.concatenate([state_ssm_conv[i], xbc_s], axis=1)[:, -(SSM_CONV - 1):][None]
            outs["ccv_s"] = ccv_new_s[:, CCV_HALO - (CCV_WIDTH - 1):][None]
        else:
            w_in = fox_w_in[i]
            w_main = w_in[:, :3 * FOX_WIDTH].astype(BF16)
            w_f = _pad_lanes(w_in[:, 3 * FOX_WIDTH:]).astype(BF16)
            b_f = _pad_lanes(fox_b_f[i][None])
            w_out = fox_w_out[i].astype(BF16)
            q_scale = FOX_HEAD_DIM ** -0.5

            q, k, v, fl = nm_matmul(yp, g_mix, sh_p, sc_p, w_main,
                                    [(FOX_WIDTH, BF16, q_scale), (FOX_WIDTH, F32, None),
                                     (FOX_WIDTH, F32, None)], w_aux=w_f)
            lf, cum, cum_t = logf_cumsum(fl, b_f)
            att = fox_prompt_attention(q, k, v, cum, cum_t.reshape(bp * LANES, 1, seq))
            yp = mm_residual([att], w_out, yp, gt_p)
            outs["k_p"] = k.reshape(1, bp, seq, FOX_HEADS, FOX_HEAD_DIM)
            outs["v_p"] = v.reshape(1, bp, seq, FOX_HEADS, FOX_HEAD_DIM)
            outs["lf_p"] = lf[:, :, :FOX_HEADS][None]

            q, k, v, fl = nm_matmul(ys, g_mix, sh_s, sc_s, w_main,
                                    [(FOX_WIDTH, F32, None)] * 3, w_aux=w_f)
            lf, _, _ = logf_cumsum(fl, b_f)
            lf_s = lf.reshape(bs, dec_seq, LANES)[:, :, :FOX_HEADS]
            n_new = dec_seq * FOX_HEADS
            head_rows = lambda t: t.reshape(bs, n_new, FOX_HEAD_DIM)
            pad_page = lambda t: jnp.pad(head_rows(t), ((0, 0), (0, LANES - n_new), (0, 0)))
            lf_new = jnp.pad(lf_s.reshape(bs, 1, n_new), ((0, 0), (0, SUBLANES - 1), (0, LANES - n_new)))
            n_pool = cache_k.shape[1]
            page_rows = PAGE_SIZE * FOX_HEADS
            att = fox_decode_attention(
                head_rows(q), pad_page(k), pad_page(v), lf_new,
                cache_k[i].reshape(n_pool, page_rows, FOX_HEAD_DIM),
                cache_v[i].reshape(n_pool, page_rows, FOX_HEAD_DIM),
                cache_logf[i].reshape(n_pool, page_rows // LANES, LANES), page_table)
            ys = mm_residual([att.reshape(1, ts, FOX_WIDTH)], w_out, ys, gt_s)
            outs["k_s"] = k.reshape(1, bs, dec_seq, FOX_HEADS, FOX_HEAD_DIM)
            outs["v_s"] = v.reshape(1, bs, dec_seq, FOX_HEADS, FOX_HEAD_DIM)
            outs["lf_s"] = lf_s[None]

        (sh_p, sc_p, gt_p), (sh_s, sc_s, gt_s) = _mods(mod[2 * layer + 1, :bp], mod[2 * layer + 1, bp:n_c],
                                                       dec_seq)
        g_moe = norm_g[layer, 1][None]
        w_router = _pad_lanes(jnp.concatenate([moe_w_expert[layer], moe_w_group[layer]], axis=1))
        b_router = _pad_lanes(jnp.concatenate([moe_b_expert[layer], moe_b_group[layer]])[None])
        w_gu = jnp.concatenate([moe_w_gate[layer], moe_w_up[layer]], axis=-1).astype(BF16)
        w_d = moe_w_down[layer].astype(BF16)
        yp = _moe_layer(yp, sh_p, sc_p, gt_p, g_moe, w_router, b_router, w_gu, w_d)
        ys = _moe_layer(ys, sh_s, sc_s, gt_s, g_moe, w_router, b_router, w_gu, w_d)

    y_prompt = final_rmsnorm(yp.reshape(bp * seq, d), final_g[None]).reshape(bp, seq, d)
    y_sample = final_rmsnorm(ys.reshape(ts, d), final_g[None]).reshape(bs, dec_seq, d)
    return (y_prompt, y_sample, outs["k_p"], outs["v_p"], outs["lf_p"], outs["k_s"], outs["v_s"],
            outs["lf_s"], outs["ssm_state_p"], outs["ssm_state_s"], outs["ssm_conv_p"],
            outs["ssm_conv_s"], outs["ccv_p"], outs["ccv_s"])
```
